```python
import math
import jax
import jax.numpy as jnp
from jax import lax
import numpy as np

D_MODEL = 1024
BATCH = 4
SEQ = 4096
DEPTH = 4
DEC_BATCH = 128
DEC_SEQ = 1
PAST_LEN = 2048
PAGE_SIZE = 128

N_MIXERS = 4
N_LAYERS_CONV = (DEPTH + N_MIXERS - 1) // N_MIXERS
N_LAYERS_MOBA = (DEPTH + N_MIXERS - 2) // N_MIXERS
N_LAYERS_S5 = (DEPTH + N_MIXERS - 3) // N_MIXERS
N_LAYERS_DIL = DEPTH // N_MIXERS
N_HEADS = 16
HEAD_DIM = D_MODEL // N_HEADS
ROT_DIM = HEAD_DIM // 4
ROPE_THETA = 500000.0
EPS = 1e-6
PLE_DIM = 256
CONV_A_WIDTH = 31
MOBA_BLOCK = 256
MOBA_TOPK = 3
MOBA_Q_CHUNK = 64
S5_GROUP = 16
S5_GROUPS = D_MODEL // S5_GROUP
S5_STATE = 64
S5_DT_MIN = 1e-3
S5_DT_MAX = 1e-1
DIL_PAIRS = ((128, 1), (512, 4), (2048, 16))
N_DIL = len(DIL_PAIRS)
DIL_Q_CHUNK = 64
D_FF = 2816
FFN_CONV_WIDTH = 3

STATE_KEYS = ('conv_a', 'moba_k', 'moba_v', 's5_re', 's5_im', 'win_k1', 'win_v1', 'win_k2', 'win_v2', 'win_k3', 'win_v3', 'ffn_conv')
WIN_KEYS = ('win_k1', 'win_v1', 'win_k2', 'win_v2', 'win_k3', 'win_v3')

kernel_name = 'hybrid_conv_moba_s5_dilated_decode_step'


def rmsnorm(x, g):
    xf = x.astype(jnp.float32)
    y = xf * lax.rsqrt(jnp.mean(xf * xf, axis=-1, keepdims=True) + EPS)
    return (y * g.astype(jnp.float32)).astype(x.dtype)


def rope_partial(x, pos):
    half = ROT_DIM // 2
    inv_freq = ROPE_THETA ** (-jnp.arange(half, dtype=jnp.float32) / half)
    ang = pos.astype(jnp.float32)[:, None] * inv_freq[None, :]
    cos = jnp.cos(ang)[None, :, None, :]
    sin = jnp.sin(ang)[None, :, None, :]
    xf = x.astype(jnp.float32)
    x1 = xf[..., :half]
    x2 = xf[..., half:ROT_DIM]
    out = jnp.concatenate([x1 * cos - x2 * sin, x2 * cos + x1 * sin, xf[..., ROT_DIM:]], axis=-1)
    return out.astype(x.dtype)


def causal_dwconv(u, w, buf):
    ext = jnp.concatenate([buf.astype(u.dtype), u], axis=1)
    out = lax.conv_general_dilated(ext, w[:, None, :].astype(u.dtype), window_strides=(1,), padding='VALID',
                                   dimension_numbers=('NWC', 'WIO', 'NWC'), feature_group_count=u.shape[-1])
    return out, ext[:, ext.shape[1] - (w.shape[0] - 1):]


def conformer_conv_mixer(h, buf, w_in, b_in, w_dw, b_dw, ln_g, ln_b, w_out):
    z = h @ w_in + b_in
    u = z[..., :D_MODEL] * jax.nn.sigmoid(z[..., D_MODEL:])
    c, new_buf = causal_dwconv(u, w_dw, buf)
    cf = (c + b_dw).astype(jnp.float32)
    mu = jnp.mean(cf, axis=-1, keepdims=True)
    var = jnp.mean(jnp.square(cf - mu), axis=-1, keepdims=True)
    cn = ((cf - mu) * lax.rsqrt(var + EPS) * ln_g.astype(jnp.float32) + ln_b.astype(jnp.float32)).astype(h.dtype)
    return jax.nn.silu(cn) @ w_out, new_buf


def moba_attend(q, q_pos, k, v, kmean):
    B, Q, H, dh = q.shape
    own = q_pos // MOBA_BLOCK
    n_blk = kmean.shape[1]
    s_blk = jnp.einsum('bqhd,bnhd->bqhn', q, kmean).astype(jnp.float32)
    past = jnp.arange(n_blk)[None, :] < own[:, None]
    s_blk = jnp.where(past[None, :, None, :], s_blk, -jnp.inf)
    _, top = lax.top_k(s_blk, MOBA_TOPK)
    own_b = jnp.broadcast_to(own[None, :, None, None], (B, Q, H, 1)).astype(top.dtype)
    blk = jnp.concatenate([top, own_b], axis=-1)
    blk_ok = jnp.concatenate([top < own[None, :, None, None], jnp.ones((B, Q, H, 1), dtype=bool)], axis=-1)
    key_pos = blk[..., None] * MOBA_BLOCK + jnp.arange(MOBA_BLOCK)
    bi = jnp.arange(B)[:, None, None, None, None]
    hi = jnp.arange(H)[None, None, :, None, None]
    kg = k[bi, key_pos, hi]
    vg = v[bi, key_pos, hi]
    s = jnp.einsum('bqhd,bqhnkd->bqhnk', q, kg).astype(jnp.float32) * (HEAD_DIM ** -0.5)
    ok = blk_ok[..., None] & (key_pos <= q_pos[None, :, None, None, None])
    s = jnp.where(ok, s, -jnp.inf).reshape(B, Q, H, -1)
    pr = jax.nn.softmax(s, axis=-1).reshape(B, Q, H, MOBA_TOPK + 1, MOBA_BLOCK)
    return jnp.einsum('bqhnk,bqhnkd->bqhd', pr.astype(vg.dtype), vg)


def moba_mixer(h, pos, past_k, past_v, w_qkv, w_o, q_chunk):
    B, S, _ = h.shape
    qkv = (h @ w_qkv).reshape(B, S, 3, N_HEADS, HEAD_DIM)
    q = rope_partial(qkv[:, :, 0], pos)
    k = rope_partial(qkv[:, :, 1], pos)
    v = qkv[:, :, 2]
    n_past = 0 if past_k is None else past_k.shape[1]
    L = n_past + S
    n_blk = max(-(-L // MOBA_BLOCK), MOBA_TOPK)
    pad = jnp.zeros((B, n_blk * MOBA_BLOCK - L, N_HEADS, HEAD_DIM), k.dtype)
    k_parts = [k, pad] if past_k is None else [past_k.astype(k.dtype), k, pad]
    v_parts = [v, pad] if past_v is None else [past_v.astype(v.dtype), v, pad]
    k_all = jnp.concatenate(k_parts, axis=1)
    v_all = jnp.concatenate(v_parts, axis=1)
    kmean = jnp.mean(k_all.astype(jnp.float32).reshape(B, n_blk, MOBA_BLOCK, N_HEADS, HEAD_DIM), axis=2)
    n_chunk = S // q_chunk
    qc = q.reshape(B, n_chunk, q_chunk, N_HEADS, HEAD_DIM).swapaxes(0, 1)
    pc = pos.reshape(n_chunk, q_chunk)
    o = lax.map(lambda a: moba_attend(a[0], a[1], k_all, v_all, kmean), (qc, pc))
    o = o.swapaxes(0, 1).reshape(B, S, D_MODEL).astype(h.dtype)
    return o @ w_o, k, v


def _ssm_combine(e1, e2):
    a1r, a1i, b1r, b1i = e1
    a2r, a2i, b2r, b2i = e2
    return (a2r * a1r - a2i * a1i, a2r * a1i + a2i * a1r,
            a2r * b1r - a2i * b1i + b2r, a2r * b1i + a2i * b1r + b2i)


def s5_mixer(h, h0_re, h0_im, a_re, a_im, log_dt, b_re, b_im, c_re, c_im, d, w_glu):
    B, S, _ = h.shape
    f32 = jnp.float32
    a_re = a_re.astype(f32)
    a_im = a_im.astype(f32)
    dt = jnp.exp(log_dt.astype(f32))[:, None]
    mag = jnp.exp(a_re * dt)
    abar_re = mag * jnp.cos(a_im * dt)
    abar_im = mag * jnp.sin(a_im * dt)
    den = a_re * a_re + a_im * a_im
    f_re = ((abar_re - 1.0) * a_re + abar_im * a_im) / den
    f_im = (abar_im * a_re - (abar_re - 1.0) * a_im) / den
    br = b_re.astype(f32)
    bim = b_im.astype(f32)
    bbar_re = f_re[..., None] * br - f_im[..., None] * bim
    bbar_im = f_re[..., None] * bim + f_im[..., None] * br
    u = h.astype(f32).reshape(B, S, S5_GROUPS, S5_GROUP)
    bu_re = jnp.einsum('gpc,bsgc->bsgp', bbar_re, u)
    bu_im = jnp.einsum('gpc,bsgc->bsgp', bbar_im, u)
    shp = bu_re.shape
    pw_re, pw_im, x_re, x_im = lax.associative_scan(
        _ssm_combine, (jnp.broadcast_to(abar_re, shp), jnp.broadcast_to(abar_im, shp), bu_re, bu_im), axis=1)
    h0r = h0_re.astype(f32)[:, None]
    h0i = h0_im.astype(f32)[:, None]
    x_re = x_re + pw_re * h0r - pw_im * h0i
    x_im = x_im + pw_re * h0i + pw_im * h0r
    y = jnp.einsum('gcp,bsgp->bsgc', c_re.astype(f32), x_re) - jnp.einsum('gcp,bsgp->bsgc', c_im.astype(f32), x_im)
    y = y.reshape(B, S, D_MODEL) + d.astype(f32) * h.astype(f32)
    z = jax.nn.gelu(y)
    g = z @ w_glu.astype(f32)
    out = g[..., :D_MODEL] * jax.nn.sigmoid(g[..., D_MODEL:])
    return out.astype(h.dtype), x_re[:, -1], x_im[:, -1]


def dilated_attend(q, q_pos, kbuf, vbuf, base, dil, n_taps):
    key_pos = q_pos[:, None] - dil * jnp.arange(n_taps)[None, :]
    valid = key_pos >= base
    idx = jnp.clip(key_pos - base, 0, kbuf.shape[1] - 1)
    kg = kbuf[:, idx]
    vg = vbuf[:, idx]
    s = jnp.einsum('bqhd,bqthd->bqht', q, kg).astype(jnp.float32) * (HEAD_DIM ** -0.5)
    s = jnp.where(valid[None, :, None, :], s, -jnp.inf)
    m = jnp.max(s, axis=-1, keepdims=True)
    e = jnp.exp(s - m)
    den = jnp.sum(e, axis=-1, keepdims=True)
    o = jnp.einsum('bqht,bqthd->bqhd', e / den, vg.astype(jnp.float32))
    return o, (m + jnp.log(den))[..., 0]


def dilated_mixer(h, pos, pos0, prev, w_qkv, w_o, q_chunk):
    B, S, _ = h.shape
    qkv = (h @ w_qkv).reshape(B, S, N_DIL, 3, N_HEADS, HEAD_DIM)
    qs, kbufs, vbufs, bases, new_bufs = [], [], [], [], []
    for g, (win, _) in enumerate(DIL_PAIRS):
        q = rope_partial(qkv[:, :, g, 0], pos)
        k = rope_partial(qkv[:, :, g, 1], pos)
        v = qkv[:, :, g, 2]
        if prev is not None:
            k = jnp.concatenate([prev[2 * g].astype(k.dtype), k], axis=1)
            v = jnp.concatenate([prev[2 * g + 1].astype(v.dtype), v], axis=1)
        n_rows = k.shape[1]
        keep = min(win, n_rows)
        qs.append(q)
        kbufs.append(k)
        vbufs.append(v)
        bases.append(pos0 + S - n_rows)
        new_bufs += [k[:, n_rows - keep:], v[:, n_rows - keep:]]
    n_chunk = S // q_chunk
    qc = jnp.stack(qs, 0).reshape(N_DIL, B, n_chunk, q_chunk, N_HEADS, HEAD_DIM).transpose(2, 0, 1, 3, 4, 5)
    pc = pos.reshape(n_chunk, q_chunk)

    def chunk(args):
        qg, pg = args
        outs, lses = [], []
        for g, (win, dil) in enumerate(DIL_PAIRS):
            o, l = dilated_attend(qg[g], pg, kbufs[g], vbufs[g], bases[g], dil, win // dil + 1)
            outs.append(o)
            lses.append(l)
        wts = jax.nn.softmax(jnp.stack(lses, 0), axis=0)
        return jnp.einsum('gbqh,gbqhd->bqhd', wts, jnp.stack(outs, 0))

    o = lax.map(chunk, (qc, pc))
    o = o.transpose(1, 0, 2, 3, 4).reshape(B, S, D_MODEL).astype(h.dtype)
    return o @ w_o, new_bufs


def conv_ffn(h, buf, w_up, w_conv, w_down):
    u = h @ w_up
    c, new_buf = causal_dwconv(u, w_conv, buf)
    return (jax.nn.silu(c[..., :D_FF]) * c[..., D_FF:]) @ w_down, new_buf


def per_layer_embedding(x, p_i, g, w_proj, w_gate):
    gate = jax.nn.sigmoid(rmsnorm(x, g) @ w_gate)
    return (p_i @ w_proj) * gate


def trunk(x, p, pos0, st, W):
    B, S, _ = x.shape
    pos = pos0 + jnp.arange(S, dtype=jnp.int32)
    new = {n: [] for n in STATE_KEYS}
    for i in range(DEPTH):
        kind, j = i % N_MIXERS, i // N_MIXERS
        h = rmsnorm(x, W['norm_mix'][i])
        if kind == 0:
            buf = jnp.zeros((B, CONV_A_WIDTH - 1, D_MODEL), x.dtype) if st is None else st['conv_a'][j]
            y, nb = conformer_conv_mixer(h, buf, W['a_w_in'][j], W['a_b_in'][j], W['a_w_dw'][j], W['a_b_dw'][j],
                                         W['a_ln_g'][j], W['a_ln_b'][j], W['a_w_out'][j])
            new['conv_a'].append(nb)
        elif kind == 1:
            if st is None:
                pk, pv, qc = None, None, MOBA_Q_CHUNK
            else:
                pt = st['page_table']
                pk = st['moba_k'][j, pt].reshape(B, -1, N_HEADS, HEAD_DIM)
                pv = st['moba_v'][j, pt].reshape(B, -1, N_HEADS, HEAD_DIM)
                qc = S
            y, kn, vn = moba_mixer(h, pos, pk, pv, W['b_w_qkv'][j], W['b_w_o'][j], qc)
            new['moba_k'].append(kn)
            new['moba_v'].append(vn)
        elif kind == 2:
            if st is None:
                h0r = jnp.zeros((B, S5_GROUPS, S5_STATE), jnp.float32)
                h0i = jnp.zeros((B, S5_GROUPS, S5_STATE), jnp.float32)
            else:
                h0r, h0i = st['s5_re'][j], st['s5_im'][j]
            y, sr, si = s5_mixer(h, h0r, h0i, W['c_a_re'][j], W['c_a_im'][j], W['c_log_dt'][j], W['c_b_re'][j],
                                 W['c_b_im'][j], W['c_c_re'][j], W['c_c_im'][j], W['c_d'][j], W['c_w_glu'][j])
            new['s5_re'].append(sr)
            new['s5_im'].append(si)
        else:
            prev = None if st is None else [st[n][j] for n in WIN_KEYS]
            qc = DIL_Q_CHUNK if st is None else S
            y, bufs = dilated_mixer(h, pos, pos0, prev, W['d_w_qkv'][j], W['d_w_o'][j], qc)
            for n, b in zip(WIN_KEYS, bufs):
                new[n].append(b)
        x = x + y
        fb = jnp.zeros((B, FFN_CONV_WIDTH - 1, 2 * D_FF), x.dtype) if st is None else st['ffn_conv'][i]
        y, nb = conv_ffn(rmsnorm(x, W['norm_ffn'][i]), fb, W['f_w_up'][i], W['f_w_conv'][i], W['f_w_down'][i])
        new['ffn_conv'].append(nb)
        x = x + y
        x = x + per_layer_embedding(x, p[i], W['norm_ple'][i], W['ple_w_proj'][i], W['ple_w_gate'][i])
    return rmsnorm(x, W['norm_final']), {n: jnp.stack(v, 0) for n, v in new.items()}


def setup_inputs(seed: int = 0) -> dict:
    key = jax.random.key(seed)
    ks = iter(jax.random.split(key, 64))

    def nrm(shape, scale):
        return scale * jax.random.normal(next(ks), shape, jnp.float32)

    D, F = D_MODEL, D_FF
    n_pages = PAST_LEN // PAGE_SIZE
    n_used = DEC_BATCH * n_pages
    n_phys = n_used + max(1, n_used // 4)
    perm = jax.random.permutation(next(ks), n_phys)
    page_table = perm[:n_used].reshape(DEC_BATCH, n_pages).astype(jnp.int32)
    wr = [min(w, PAST_LEN) for w, _ in DIL_PAIRS]
    kv_shape = (N_LAYERS_MOBA, n_phys, PAGE_SIZE, N_HEADS, HEAD_DIM)
    return {
        'x_prompt': nrm((BATCH, SEQ, D), 1.0),
        'x_sample': nrm((DEC_BATCH, DEC_SEQ, D), 1.0),
        'state_conv_a': nrm((N_LAYERS_CONV, DEC_BATCH, CONV_A_WIDTH - 1, D), 0.5),
        'cache_moba_k': nrm(kv_shape, 1.0),
        'cache_moba_v': nrm(kv_shape, 1.0),
        'state_s5_re': nrm((N_LAYERS_S5, DEC_BATCH, S5_GROUPS, S5_STATE), 0.5),
        'state_s5_im': nrm((N_LAYERS_S5, DEC_BATCH, S5_GROUPS, S5_STATE), 0.5),
        'cache_win_k1': nrm((N_LAYERS_DIL, DEC_BATCH, wr[0], N_HEADS, HEAD_DIM), 1.0),
        'cache_win_v1': nrm((N_LAYERS_DIL, DEC_BATCH, wr[0], N_HEADS, HEAD_DIM), 1.0),
        'cache_win_k2': nrm((N_LAYERS_DIL, DEC_BATCH, wr[1], N_HEADS, HEAD_DIM), 1.0),
        'cache_win_v2': nrm((N_LAYERS_DIL, DEC_BATCH, wr[1], N_HEADS, HEAD_DIM), 1.0),
        'cache_win_k3': nrm((N_LAYERS_DIL, DEC_BATCH, wr[2], N_HEADS, HEAD_DIM), 1.0),
        'cache_win_v3': nrm((N_LAYERS_DIL, DEC_BATCH, wr[2], N_HEADS, HEAD_DIM), 1.0),
        'state_ffn_conv': nrm((DEPTH, DEC_BATCH, FFN_CONV_WIDTH - 1, 2 * F), 0.5),
        'page_table': page_table,
        'p_prompt': nrm((DEPTH, BATCH, SEQ, PLE_DIM), 1.0),
        'p_sample': nrm((DEPTH, DEC_BATCH, DEC_SEQ, PLE_DIM), 1.0),
        'norm_mix': 1.0 + nrm((DEPTH, D), 0.01),
        'norm_ffn': 1.0 + nrm((DEPTH, D), 0.01),
        'norm_ple': 1.0 + nrm((DEPTH, D), 0.01),
        'norm_final': 1.0 + nrm((D,), 0.01),
        'a_w_in': nrm((N_LAYERS_CONV, D, 2 * D), D ** -0.5),
        'a_b_in': nrm((N_LAYERS_CONV, 2 * D), 0.01),
        'a_w_dw': nrm((N_LAYERS_CONV, CONV_A_WIDTH, D), CONV_A_WIDTH ** -0.5),
        'a_b_dw': nrm((N_LAYERS_CONV, D), 0.01),
        'a_ln_g': 1.0 + nrm((N_LAYERS_CONV, D), 0.01),
        'a_ln_b': nrm((N_LAYERS_CONV, D), 0.01),
        'a_w_out': nrm((N_LAYERS_CONV, D, D), D ** -0.5),
        'b_w_qkv': nrm((N_LAYERS_MOBA, D, 3 * D), D ** -0.5),
        'b_w_o': nrm((N_LAYERS_MOBA, D, D), D ** -0.5),
        'c_a_re': -0.5 + nrm((N_LAYERS_S5, S5_GROUPS, S5_STATE), 0.01),
        'c_a_im': math.pi * jnp.arange(S5_STATE, dtype=jnp.float32) + nrm((N_LAYERS_S5, S5_GROUPS, S5_STATE), 0.01),
        'c_log_dt': jax.random.uniform(next(ks), (N_LAYERS_S5, S5_GROUPS), jnp.float32,
                                       minval=math.log(S5_DT_MIN), maxval=math.log(S5_DT_MAX)),
        'c_b_re': nrm((N_LAYERS_S5, S5_GROUPS, S5_STATE, S5_GROUP), (2 * S5_GROUP) ** -0.5),
        'c_b_im': nrm((N_LAYERS_S5, S5_GROUPS, S5_STATE, S5_GROUP), (2 * S5_GROUP) ** -0.5),
        'c_c_re': nrm((N_LAYERS_S5, S5_GROUPS, S5_GROUP, S5_STATE), (2 * S5_STATE) ** -0.5),
        'c_c_im': nrm((N_LAYERS_S5, S5_GROUPS, S5_GROUP, S5_STATE), (2 * S5_STATE) ** -0.5),
        'c_d': nrm((N_LAYERS_S5, D), 0.5),
        'c_w_glu': nrm((N_LAYERS_S5, D, 2 * D), D ** -0.5),
        'd_w_qkv': nrm((N_LAYERS_DIL, D, N_DIL * 3 * N_HEADS * HEAD_DIM), D ** -0.5),
        'd_w_o': nrm((N_LAYERS_DIL, D, D), D ** -0.5),
        'f_w_up': nrm((DEPTH, D, 2 * F), D ** -0.5),
        'f_w_conv': nrm((DEPTH, FFN_CONV_WIDTH, 2 * F), FFN_CONV_WIDTH ** -0.5),
        'f_w_down': nrm((DEPTH, F, D), F ** -0.5),
        'ple_w_proj': nrm((DEPTH, PLE_DIM, D), PLE_DIM ** -0.5),
        'ple_w_gate': nrm((DEPTH, D, D), D ** -0.5),
    }


def reference(x_prompt, x_sample, state_conv_a, cache_moba_k, cache_moba_v, state_s5_re, state_s5_im,
              cache_win_k1, cache_win_v1, cache_win_k2, cache_win_v2, cache_win_k3, cache_win_v3,
              state_ffn_conv, page_table, p_prompt, p_sample,
              norm_mix, norm_ffn, norm_ple, norm_final,
              a_w_in, a_b_in, a_w_dw, a_b_dw, a_ln_g, a_ln_b, a_w_out,
              b_w_qkv, b_w_o,
              c_a_re, c_a_im, c_log_dt, c_b_re, c_b_im, c_c_re, c_c_im, c_d, c_w_glu,
              d_w_qkv, d_w_o,
              f_w_up, f_w_conv, f_w_down,
              ple_w_proj, ple_w_gate):
    W = dict(norm_mix=norm_mix, norm_ffn=norm_ffn, norm_ple=norm_ple, norm_final=norm_final,
             a_w_in=a_w_in, a_b_in=a_b_in, a_w_dw=a_w_dw, a_b_dw=a_b_dw, a_ln_g=a_ln_g, a_ln_b=a_ln_b,
             a_w_out=a_w_out, b_w_qkv=b_w_qkv, b_w_o=b_w_o,
             c_a_re=c_a_re, c_a_im=c_a_im, c_log_dt=c_log_dt, c_b_re=c_b_re, c_b_im=c_b_im,
             c_c_re=c_c_re, c_c_im=c_c_im, c_d=c_d, c_w_glu=c_w_glu,
             d_w_qkv=d_w_qkv, d_w_o=d_w_o, f_w_up=f_w_up, f_w_conv=f_w_conv, f_w_down=f_w_down,
             ple_w_proj=ple_w_proj, ple_w_gate=ple_w_gate)
    st = dict(conv_a=state_conv_a, moba_k=cache_moba_k, moba_v=cache_moba_v, s5_re=state_s5_re, s5_im=state_s5_im,
              win_k1=cache_win_k1, win_v1=cache_win_v1, win_k2=cache_win_k2, win_v2=cache_win_v2,
              win_k3=cache_win_k3, win_v3=cache_win_v3, ffn_conv=state_ffn_conv, page_table=page_table)
    past_len = page_table.shape[1] * PAGE_SIZE
    y_prompt, sp = trunk(x_prompt, p_prompt, 0, None, W)
    y_sample, ss = trunk(x_sample, p_sample, past_len, st, W)
    return (y_prompt, y_sample,
            sp['conv_a'], sp['moba_k'], sp['moba_v'], sp['s5_re'], sp['s5_im'],
            sp['win_k1'], sp['win_v1'], sp['win_k2'], sp['win_v2'], sp['win_k3'], sp['win_v3'], sp['ffn_conv'],
            ss['conv_a'], ss['moba_k'], ss['moba_v'], ss['s5_re'], ss['s5_im'],
            ss['win_k1'], ss['win_v1'], ss['win_k2'], ss['win_v2'], ss['win_k3'], ss['win_v3'], ss['ffn_conv'])
```

```python
import functools
import math

import jax
import jax.numpy as jnp
from jax import lax
from jax.experimental import pallas as pl
from jax.experimental.pallas import tpu as pltpu

F32 = jnp.float32
BF16 = jnp.bfloat16

D_MODEL = 1024
N_HEADS = 16
HEAD_DIM = 64
ROT_DIM = 16
ROPE_THETA = 500000.0
EPS = 1e-6
PLE_DIM = 256
CONV_A_WIDTH = 31
MOBA_BLOCK = 256
MOBA_TOPK = 3
S5_GROUP = 16
S5_GROUPS = 64
S5_STATE = 64
S5_DIM = S5_GROUPS * S5_STATE
DIL_PAIRS = ((128, 1), (512, 4), (2048, 16))
N_DIL = 3
DIL_TAPS = 128
D_FF = 2816
PAGE_SIZE = 128

LANE = 128
SUBLANE = 8
FF_CHUNK = 256
N_FF_CHUNK = D_FF // FF_CHUNK
VMEM_LIMIT = 56 * 1024 * 1024
MASK_NEG = -30000.0

_NT = (((1,), (1,)), ((), ()))


def _cparams(*sem):
    return pltpu.CompilerParams(dimension_semantics=sem, vmem_limit_bytes=VMEM_LIMIT)


def _resident(shape):
    nd = len(shape)
    return pl.BlockSpec(shape, lambda *_: (0,) * nd, pipeline_mode=pl.Buffered(1))


def _rms(x, g):
    return x * lax.rsqrt(jnp.mean(x * x, axis=-1, keepdims=True) + EPS) * g


def _dot(a, b):
    return jnp.dot(a, b, preferred_element_type=F32)


def _split_dot(a, b):
    hi = a.astype(BF16)
    lo = (a - hi.astype(F32)).astype(BF16)
    return _dot(hi, b) + _dot(lo, b)


def _row_tile(m):
    return 512 if m % 512 == 0 else m


def _conv_in_kernel(x_ref, g_ref, w_ref, b_ref, u_ref):
    h = _rms(x_ref[...], g_ref[...]).astype(BF16)
    z = _dot(h, w_ref[...]) + b_ref[...]
    u_ref[...] = z[:, :D_MODEL] * jax.nn.sigmoid(z[:, D_MODEL:])


def _conv_in(x, g, w, b):
    m = x.shape[0]
    tm = _row_tile(m)
    return pl.pallas_call(
        _conv_in_kernel,
        grid=(m // tm,),
        in_specs=[pl.BlockSpec((tm, D_MODEL), lambda i: (i, 0)),
                  _resident((1, D_MODEL)), _resident((D_MODEL, 2 * D_MODEL)), _resident((1, 2 * D_MODEL))],
        out_specs=pl.BlockSpec((tm, D_MODEL), lambda i: (i, 0)),
        out_shape=jax.ShapeDtypeStruct((m, D_MODEL), F32),
        compiler_params=_cparams("parallel"),
        name="conv_in",
    )(x, g, w, b)


def _ln_silu_out(cf, x, lng, lnb, wout):
    mu = jnp.mean(cf, axis=-1, keepdims=True)
    var = jnp.mean(jnp.square(cf - mu), axis=-1, keepdims=True)
    cn = (cf - mu) * lax.rsqrt(var + EPS) * lng + lnb
    return x + _dot(jax.nn.silu(cn).astype(BF16), wout)


CONV_HALO = 32


def _conv_out_kernel(u_ref, halo_ref, x_ref, wdw_ref, bdw_ref, lng_ref, lnb_ref, wout_ref, o_ref,
                     ext_ref, cf_ref, *, tm, tiles_per_seq):
    first = (pl.program_id(0) % tiles_per_seq) == 0
    ext_ref[0:CONV_HALO, :] = jnp.where(first, 0.0, halo_ref[...])
    ext_ref[CONV_HALO:, :] = u_ref[...]
    off = CONV_HALO - (CONV_A_WIDTH - 1)
    for c in range(D_MODEL // LANE):
        cols = slice(c * LANE, (c + 1) * LANE)
        acc = jnp.zeros((tm, LANE), F32)
        for j in range(CONV_A_WIDTH):
            acc = acc + wdw_ref[j:j + 1, cols] * ext_ref[pl.ds(off + j, tm), cols]
        cf_ref[:, cols] = acc + bdw_ref[:, cols]
    o_ref[...] = _ln_silu_out(cf_ref[...], x_ref[...], lng_ref[...], lnb_ref[...], wout_ref[...])


def _conv_out(u, x, seq, wdw, bdw, lng, lnb, wout):
    m = x.shape[0]
    tm = 256
    tps = seq // tm
    hb = tm // CONV_HALO
    kern = functools.partial(_conv_out_kernel, tm=tm, tiles_per_seq=tps)
    return pl.pallas_call(
        kern,
        grid=(m // tm,),
        in_specs=[pl.BlockSpec((tm, D_MODEL), lambda i: (i, 0)),
                  pl.BlockSpec((CONV_HALO, D_MODEL), lambda i: (jnp.maximum(i * hb - 1, 0), 0)),
                  pl.BlockSpec((tm, D_MODEL), lambda i: (i, 0)),
                  _resident((CONV_HALO, D_MODEL)), _resident((1, D_MODEL)), _resident((1, D_MODEL)),
                  _resident((1, D_MODEL)), _resident((D_MODEL, D_MODEL))],
        out_specs=pl.BlockSpec((tm, D_MODEL), lambda i: (i, 0)),
        out_shape=jax.ShapeDtypeStruct((m, D_MODEL), F32),
        scratch_shapes=[pltpu.VMEM((tm + CONV_HALO, D_MODEL), F32), pltpu.VMEM((tm, D_MODEL), F32)],
        compiler_params=_cparams("parallel"),
        name="conv_out",
    )(u, u, x, wdw, bdw, lng, lnb, wout)


def _conv_dec_kernel(x_ref, g_ref, win_ref, bin_ref, buf_ref, wdw_ref, bdw_ref, lng_ref, lnb_ref, wout_ref,
                     o_ref, u_ref, acc_ref):
    j = pl.program_id(0)

    @pl.when(j == 0)
    def _():
        h = _rms(x_ref[...], g_ref[...]).astype(BF16)
        z = _dot(h, win_ref[...]) + bin_ref[...]
        u = z[:, :D_MODEL] * jax.nn.sigmoid(z[:, D_MODEL:])
        u_ref[...] = u
        acc_ref[...] = wdw_ref[CONV_A_WIDTH - 1:CONV_A_WIDTH, :] * u

    acc_ref[...] += wdw_ref[pl.ds(j, 1), :] * buf_ref[0]

    @pl.when(j == CONV_A_WIDTH - 2)
    def _():
        cf = acc_ref[...] + bdw_ref[...]
        o_ref[...] = _ln_silu_out(cf, x_ref[...], lng_ref[...], lnb_ref[...], wout_ref[...])


def _conv_dec(x, g, win, bin_, buf_t, wdw, bdw, lng, lnb, wout):
    m = x.shape[0]
    return pl.pallas_call(
        _conv_dec_kernel,
        grid=(CONV_A_WIDTH - 1,),
        in_specs=[_resident((m, D_MODEL)), _resident((1, D_MODEL)), _resident((D_MODEL, 2 * D_MODEL)),
                  _resident((1, 2 * D_MODEL)),
                  pl.BlockSpec((1, m, D_MODEL), lambda j: (j, 0, 0)),
                  _resident((CONV_HALO, D_MODEL)), _resident((1, D_MODEL)), _resident((1, D_MODEL)),
                  _resident((1, D_MODEL)), _resident((D_MODEL, D_MODEL))],
        out_specs=[pl.BlockSpec((m, D_MODEL), lambda j: (0, 0)), pl.BlockSpec((m, D_MODEL), lambda j: (0, 0))],
        out_shape=[jax.ShapeDtypeStruct((m, D_MODEL), F32), jax.ShapeDtypeStruct((m, D_MODEL), F32)],
        scratch_shapes=[pltpu.VMEM((m, D_MODEL), F32)],
        compiler_params=_cparams("arbitrary"),
        name="conv_dec",
    )(x, g, win, bin_, buf_t, wdw, bdw, lng, lnb, wout)


def _ffn_tail(x2, p_ref, gple_ref, wgate_ref, wproj_ref, gfin_ref, final):
    hn = _rms(x2, gple_ref[...]).astype(BF16)
    gate = jax.nn.sigmoid(_dot(hn, wgate_ref[...]))
    x3 = x2 + _dot(p_ref[...].astype(BF16), wproj_ref[...]) * gate
    if final:
        x3 = _rms(x3, gfin_ref[...])
    return x3


def _ffn_act_acc(ca, cb, j, wdn_ref, acc_ref):
    act = (jax.nn.silu(ca) * cb).astype(BF16)
    acc_ref[...] += _dot(act, wdn_ref[j])


def _ffn_prompt_kernel(x_ref, p_ref, gffn_ref, wup_ref, wcv_ref, wdn_ref, gple_ref, wgate_ref, wproj_ref,
                       gfin_ref, o_ref, st_ref, h_ref, acc_ref, ua_ref, ub_ref, carry_ref,
                       *, tm, tiles_per_seq, final):
    t = pl.program_id(0) % tiles_per_seq

    @pl.when(t == 0)
    def _():
        carry_ref[...] = jnp.zeros_like(carry_ref)

    h_ref[...] = _rms(x_ref[...], gffn_ref[...]).astype(BF16)
    acc_ref[...] = jnp.zeros_like(acc_ref)

    def conv_half(c, ubuf):
        u = _dot(h_ref[...], wup_ref[c])
        ubuf[0:SUBLANE, :] = carry_ref[c]
        ubuf[SUBLANE:, :] = u
        w = wcv_ref[c]
        cc = (w[0:1] * ubuf[pl.ds(SUBLANE - 2, tm), :] + w[1:2] * ubuf[pl.ds(SUBLANE - 1, tm), :]
              + w[2:3] * u)
        carry_ref[c] = ubuf[pl.ds(tm, SUBLANE), :]
        return cc

    def chunk(j, carry):
        ca = conv_half(j, ua_ref)
        cb = conv_half(j + N_FF_CHUNK, ub_ref)
        _ffn_act_acc(ca, cb, j, wdn_ref, acc_ref)
        return carry

    lax.fori_loop(0, N_FF_CHUNK, chunk, 0)

    @pl.when(t == tiles_per_seq - 1)
    def _():
        st_ref[0] = carry_ref[...]

    x2 = x_ref[...] + acc_ref[...]
    o_ref[...] = _ffn_tail(x2, p_ref, gple_ref, wgate_ref, wproj_ref, gfin_ref, final)


def _ffn_dec_kernel(x_ref, p_ref, gffn_ref, wup_ref, wcv_ref, wdn_ref, gple_ref, wgate_ref, wproj_ref,
                    gfin_ref, s0_ref, s1_ref, o_ref, u_ref, h_ref, acc_ref, *, final):
    h_ref[...] = _rms(x_ref[...], gffn_ref[...]).astype(BF16)
    acc_ref[...] = jnp.zeros_like(acc_ref)

    def conv_half(c):
        u = _dot(h_ref[...], wup_ref[c])
        u_ref[c] = u
        w = wcv_ref[c]
        return w[0:1] * s0_ref[c] + w[1:2] * s1_ref[c] + w[2:3] * u

    def chunk(j, carry):
        ca = conv_half(j)
        cb = conv_half(j + N_FF_CHUNK)
        _ffn_act_acc(ca, cb, j, wdn_ref, acc_ref)
        return carry

    lax.fori_loop(0, N_FF_CHUNK, chunk, 0)
    x2 = x_ref[...] + acc_ref[...]
    o_ref[...] = _ffn_tail(x2, p_ref, gple_ref, wgate_ref, wproj_ref, gfin_ref, final)


def _ffn_weight_specs():
    return [_resident((1, D_MODEL)),
            _resident((2 * N_FF_CHUNK, D_MODEL, FF_CHUNK)),
            _resident((2 * N_FF_CHUNK, SUBLANE, FF_CHUNK)),
            _resident((N_FF_CHUNK, FF_CHUNK, D_MODEL)),
            _resident((1, D_MODEL)), _resident((D_MODEL, D_MODEL)), _resident((PLE_DIM, D_MODEL)),
            _resident((1, D_MODEL))]


def _ffn_prompt(x, p, wts, batch, seq, final):
    m = x.shape[0]
    tm = _row_tile(seq)
    tps = seq // tm
    kern = functools.partial(_ffn_prompt_kernel, tm=tm, tiles_per_seq=tps, final=final)
    nc = 2 * N_FF_CHUNK
    out, st = pl.pallas_call(
        kern,
        grid=(m // tm,),
        in_specs=[pl.BlockSpec((tm, D_MODEL), lambda i: (i, 0)),
                  pl.BlockSpec((tm, PLE_DIM), lambda i: (i, 0))] + _ffn_weight_specs(),
        out_specs=[pl.BlockSpec((tm, D_MODEL), lambda i: (i, 0)),
                   pl.BlockSpec((1, nc, SUBLANE, FF_CHUNK), lambda i: (i // tps, 0, 0, 0))],
        out_shape=[jax.ShapeDtypeStruct((m, D_MODEL), F32),
                   jax.ShapeDtypeStruct((batch, nc, SUBLANE, FF_CHUNK), F32)],
        scratch_shapes=[pltpu.VMEM((tm, D_MODEL), BF16), pltpu.VMEM((tm, D_MODEL), F32),
                        pltpu.VMEM((tm + SUBLANE, FF_CHUNK), F32), pltpu.VMEM((tm + SUBLANE, FF_CHUNK), F32),
                        pltpu.VMEM((nc, SUBLANE, FF_CHUNK), F32)],
        compiler_params=_cparams("arbitrary"),
        name="ffn_prompt",
    )(x, p, *wts)
    new_state = st[:, :, SUBLANE - 2:, :].transpose(0, 2, 1, 3).reshape(batch, 2, 2 * D_FF)
    return out, new_state


def _ffn_dec(x, p, wts, state, final):
    m = x.shape[0]
    nc = 2 * N_FF_CHUNK
    kern = functools.partial(_ffn_dec_kernel, final=final)
    s = state.reshape(m, 2, nc, FF_CHUNK).transpose(1, 2, 0, 3)
    out, u = pl.pallas_call(
        kern,
        grid=(1,),
        in_specs=[_resident((m, D_MODEL)), _resident((m, PLE_DIM))] + _ffn_weight_specs()
        + [_resident((nc, m, FF_CHUNK)), _resident((nc, m, FF_CHUNK))],
        out_specs=[pl.BlockSpec((m, D_MODEL), lambda i: (0, 0)),
                   pl.BlockSpec((nc, m, FF_CHUNK), lambda i: (0, 0, 0))],
        out_shape=[jax.ShapeDtypeStruct((m, D_MODEL), F32), jax.ShapeDtypeStruct((nc, m, FF_CHUNK), F32)],
        scratch_shapes=[pltpu.VMEM((m, D_MODEL), BF16), pltpu.VMEM((m, D_MODEL), F32)],
        compiler_params=_cparams("arbitrary"),
        name="ffn_dec",
    )(x, p, *wts, s[0], s[1])
    u_rows = u.transpose(1, 0, 2).reshape(m, 1, 2 * D_FF)
    return out, jnp.concatenate([state[:, 1:], u_rows], axis=1)


def _qkv_kernel(x_ref, g_ref, w_ref, c_ref, s1_ref, s2_ref, q_ref, k_ref, v_ref):
    h = _rms(x_ref[...], g_ref[...]).astype(BF16)
    reps = D_MODEL // LANE
    cos = jnp.concatenate([c_ref[...]] * reps, axis=1)
    sn1 = jnp.concatenate([s1_ref[...]] * reps, axis=1)
    sn2 = jnp.concatenate([s2_ref[...]] * reps, axis=1)
    half = ROT_DIM // 2

    def rope(z):
        return (z * cos + pltpu.roll(z, D_MODEL - half, 1) * sn1 + pltpu.roll(z, half, 1) * sn2)

    q = rope(_dot(h, w_ref[:, 0:D_MODEL]))
    q_ref[...] = (q * (HEAD_DIM ** -0.5)).astype(BF16)
    k_ref[...] = rope(_dot(h, w_ref[:, D_MODEL:2 * D_MODEL]))
    v_ref[...] = _dot(h, w_ref[:, 2 * D_MODEL:])


def _qkv(x, g, w, tabs, n_groups, seq):
    m = x.shape[0]
    tm = _row_tile(seq)
    npb = seq // tm
    row = lambda gi, i: (i, 0)
    col = lambda gi, i: (i, gi)
    tab = pl.BlockSpec((tm, LANE), lambda gi, i: (i % npb, 0))
    n = n_groups * D_MODEL
    return pl.pallas_call(
        _qkv_kernel,
        grid=(n_groups, m // tm),
        in_specs=[pl.BlockSpec((tm, D_MODEL), row), pl.BlockSpec((1, D_MODEL), lambda gi, i: (0, 0)),
                  pl.BlockSpec((D_MODEL, 3 * D_MODEL), lambda gi, i: (0, gi)), tab, tab, tab],
        out_specs=[pl.BlockSpec((tm, D_MODEL), col)] * 3,
        out_shape=[jax.ShapeDtypeStruct((m, n), BF16), jax.ShapeDtypeStruct((m, n), F32),
                   jax.ShapeDtypeStruct((m, n), F32)],
        compiler_params=_cparams("parallel", "parallel"),
        name="qkv_rope",
    )(x, g, w, *tabs)


def _rope_tables(pos):
    half = ROT_DIM // 2
    inv_freq = ROPE_THETA ** (-jnp.arange(half, dtype=F32) / half)
    ang = pos.astype(F32)[:, None] * inv_freq[None, :]
    cos, sin = jnp.cos(ang), jnp.sin(ang)
    lane = jnp.arange(LANE) % HEAD_DIM
    idx = lane % half
    c = jnp.where(lane[None, :] < ROT_DIM, cos[:, idx], 1.0)
    s1 = jnp.where(lane[None, :] < half, -sin[:, idx], 0.0)
    s2 = jnp.where((lane[None, :] >= half) & (lane[None, :] < ROT_DIM), sin[:, idx], 0.0)
    return c, s1, s2


def _proj_kernel(a_ref, x_ref, w_ref, o_ref):
    o_ref[...] = x_ref[...] + _dot(a_ref[...], w_ref[...])


def _proj(a, x, w):
    m = x.shape[0]
    tm = _row_tile(m)
    return pl.pallas_call(
        _proj_kernel,
        grid=(m // tm,),
        in_specs=[pl.BlockSpec((tm, D_MODEL), lambda i: (i, 0)), pl.BlockSpec((tm, D_MODEL), lambda i: (i, 0)),
                  _resident((D_MODEL, D_MODEL))],
        out_specs=pl.BlockSpec((tm, D_MODEL), lambda i: (i, 0)),
        out_shape=jax.ShapeDtypeStruct((m, D_MODEL), F32),
        compiler_params=_cparams("parallel"),
        name="out_proj",
    )(a, x, w)


def _moba_kernel(q_ref, k_ref, v_ref, o_ref, ka0_ref, ka1_ref, vb_ref, km_ref, *, seq):
    nb = seq // MOBA_BLOCK
    qi = pl.program_id(2)
    blk = MOBA_BLOCK

    @pl.when(qi == 0)
    def _():
        kf = k_ref[0]
        rblk = lax.broadcasted_iota(jnp.int32, (seq, LANE), 0) >> int(math.log2(blk))
        lane = lax.broadcasted_iota(jnp.int32, (seq, LANE), 1)
        ka0_ref[...] = jnp.where(lane < HEAD_DIM, kf, (lane - HEAD_DIM == rblk).astype(F32)).astype(BF16)
        ka1_ref[...] = jnp.where(lane >= HEAD_DIM, kf, (lane == rblk).astype(F32)).astype(BF16)
        vb_ref[...] = v_ref[0].astype(BF16)
        km_ref[...] = jnp.mean(kf.reshape(nb, blk, LANE), axis=1)

    q2 = q_ref[0]
    lane_q = lax.broadcasted_iota(jnp.int32, (blk, LANE), 1)
    lane_k = lax.broadcasted_iota(jnp.int32, (nb, LANE), 1)
    n_iota = lax.broadcasted_iota(jnp.int32, (nb, blk), 0)
    row = lax.broadcasted_iota(jnp.int32, (blk, blk), 0)
    colk = lax.broadcasted_iota(jnp.int32, (blk, blk), 1)
    km = km_ref[...]
    start = pl.multiple_of(qi * blk, blk)
    outs = []
    for hd, ka_ref in ((0, ka0_ref), (1, ka1_ref)):
        mine_q = (lane_q < HEAD_DIM) if hd == 0 else (lane_q >= HEAD_DIM)
        mine_k = (lane_k < HEAD_DIM) if hd == 0 else (lane_k >= HEAD_DIM)
        kmh = jnp.where(mine_k, km, 0.0)
        hi = kmh.astype(BF16)
        lo = (kmh - hi.astype(F32)).astype(BF16)
        st = (lax.dot_general(hi, q2, _NT, preferred_element_type=F32)
              + lax.dot_general(lo, q2, _NT, preferred_element_type=F32))
        s = jnp.where(n_iota < qi, st, -jnp.inf)
        pen = jnp.full((nb, blk), MASK_NEG, F32)
        for _ in range(MOBA_TOPK):
            mx = jnp.max(s, axis=0, keepdims=True)
            idx = jnp.min(jnp.where(s == mx, n_iota, nb), axis=0, keepdims=True)
            chosen = jnp.logical_and(n_iota == idx, mx > -jnp.inf)
            pen = jnp.where(chosen, 0.0, pen)
            s = jnp.where(chosen, -jnp.inf, s)
        pen = jnp.where(n_iota == qi, 0.0, pen)
        n_lo = HEAD_DIM if hd == 0 else 0
        parts = [jnp.zeros((n_lo, blk), F32)] if n_lo else []
        parts += [pen, jnp.zeros((LANE - nb - n_lo, blk), F32)]
        pen_q = jnp.concatenate(parts, axis=0).T
        qa = jnp.where(mine_q, q2, pen_q.astype(BF16))

        s0 = lax.dot_general(qa, ka_ref[pl.ds(start, blk), :], _NT, preferred_element_type=F32)
        s0 = jnp.where(colk <= row, s0, -jnp.inf)
        m0 = jnp.max(s0, axis=1, keepdims=True)
        p0 = jnp.exp(s0 - m0)
        l0 = jnp.sum(p0, axis=1, keepdims=True)
        a0 = _dot(p0.astype(BF16), vb_ref[pl.ds(start, blk), :])

        def past(n, carry, qa=qa, ka_ref=ka_ref):
            m, l, acc = carry
            off = pl.multiple_of(n * blk, blk)
            sn = lax.dot_general(qa, ka_ref[pl.ds(off, blk), :], _NT, preferred_element_type=F32)
            mn = jnp.maximum(m, jnp.max(sn, axis=1, keepdims=True))
            alpha = jnp.exp(m - mn)
            p = jnp.exp(sn - mn)
            l = alpha * l + jnp.sum(p, axis=1, keepdims=True)
            acc = alpha * acc + _dot(p.astype(BF16), vb_ref[pl.ds(off, blk), :])
            return mn, l, acc

        _, l, acc = lax.fori_loop(0, qi, past, (m0, l0, a0))
        outs.append(acc / l)
    o_ref[0] = jnp.where(lane_q < HEAD_DIM, outs[0], outs[1]).astype(BF16)


def _moba_attn(q, k, v, batch, seq):
    nhp = D_MODEL // LANE
    nq = seq // MOBA_BLOCK
    kern = functools.partial(_moba_kernel, seq=seq)
    qspec = pl.BlockSpec((1, MOBA_BLOCK, LANE), lambda b, hp, qi: (b, qi, hp))
    kvspec = pl.BlockSpec((1, seq, LANE), lambda b, hp, qi: (b, 0, hp))
    return pl.pallas_call(
        kern,
        grid=(batch, nhp, nq),
        in_specs=[qspec, kvspec, kvspec],
        out_specs=qspec,
        out_shape=jax.ShapeDtypeStruct((batch, seq, D_MODEL), BF16),
        scratch_shapes=[pltpu.VMEM((seq, LANE), BF16), pltpu.VMEM((seq, LANE), BF16),
                        pltpu.VMEM((seq, LANE), BF16), pltpu.VMEM((seq // MOBA_BLOCK, LANE), F32)],
        compiler_params=_cparams("parallel", "parallel", "arbitrary"),
        name="moba_attn",
    )(q, k, v)


def _seg_matrices():
    seg = (jnp.arange(D_MODEL)[:, None] // HEAD_DIM == jnp.arange(LANE)[None, :]).astype(BF16)
    return seg, seg.T


def _moba_dec_kernel(pt_ref, q_ref, kn_ref, vn_ref, k0_ref, k1_ref, v0_ref, v1_ref, seg_ref, segt_ref, o_ref,
                     m_ref, l_ref, sc_ref, ob_ref, *, n_blk):
    n = pl.program_id(1)
    seg, seg_t = seg_ref[...], segt_ref[...]
    q = q_ref[0].astype(F32)
    k2 = jnp.concatenate([k0_ref[0], k1_ref[0]], axis=0)
    v2 = jnp.concatenate([v0_ref[0], v1_ref[0]], axis=0)
    s = _split_dot(k2 * q, seg)
    kmean = jnp.mean(k2, axis=0, keepdims=True)
    sc = _split_dot(kmean * q, seg)
    mx = jnp.max(s, axis=0, keepdims=True)
    p = jnp.exp(s - mx)
    m_ref[pl.ds(n, 1), :] = mx
    l_ref[pl.ds(n, 1), :] = jnp.sum(p, axis=0, keepdims=True)
    sc_ref[pl.ds(n, 1), :] = sc
    pfull = _dot(p.astype(BF16), seg_t)
    ob_ref[pl.ds(n, 1), :] = jnp.sum(pfull * v2, axis=0, keepdims=True)

    @pl.when(n == n_blk - 1)
    def _():
        n_iota = lax.broadcasted_iota(jnp.int32, (n_blk, LANE), 0)
        sb = sc_ref[...]
        sel = jnp.zeros((n_blk, LANE), F32)
        for _ in range(MOBA_TOPK):
            bm = jnp.max(sb, axis=0, keepdims=True)
            idx = jnp.min(jnp.where(sb == bm, n_iota, n_blk), axis=0, keepdims=True)
            chosen = jnp.logical_and(n_iota == idx, bm > -jnp.inf)
            sel = jnp.where(chosen, 1.0, sel)
            sb = jnp.where(chosen, -jnp.inf, sb)
        s_own = _split_dot(kn_ref[0] * q, seg)
        mb = m_ref[...]
        mm = jnp.maximum(jnp.max(jnp.where(sel > 0, mb, -jnp.inf), axis=0, keepdims=True), s_own)
        w = jnp.where(sel > 0, jnp.exp(mb - mm), 0.0)
        w_own = jnp.exp(s_own - mm)
        den = jnp.sum(w * l_ref[...], axis=0, keepdims=True) + w_own
        num = (jnp.sum(_split_dot(w, seg_t) * ob_ref[...], axis=0, keepdims=True)
               + _split_dot(w_own, seg_t) * vn_ref[0])
        o_ref[0] = (num / _split_dot(den, seg_t)).astype(BF16)


def _moba_dec(q, kn, vn, cache_k, cache_v, page_table):
    b, n_pages = page_table.shape
    ppb = MOBA_BLOCK // PAGE_SIZE
    n_blk = n_pages // ppb
    pt = page_table.reshape(-1)
    row = pl.BlockSpec((1, 1, D_MODEL), lambda i, n, pt: (i, 0, 0))

    def page(off):
        return pl.BlockSpec((1, PAGE_SIZE, D_MODEL), lambda i, n, pt: (pt[i * n_pages + n * ppb + off], 0, 0))

    kern = functools.partial(_moba_dec_kernel, n_blk=n_blk)
    seg, seg_t = _seg_matrices()
    out = pl.pallas_call(
        kern,
        grid_spec=pltpu.PrefetchScalarGridSpec(
            num_scalar_prefetch=1,
            grid=(b, n_blk),
            in_specs=[row, row, row, page(0), page(1), page(0), page(1),
                      pl.BlockSpec((D_MODEL, LANE), lambda i, n, pt: (0, 0)),
                      pl.BlockSpec((LANE, D_MODEL), lambda i, n, pt: (0, 0))],
            out_specs=row,
            scratch_shapes=[pltpu.VMEM((n_blk, LANE), F32), pltpu.VMEM((n_blk, LANE), F32),
                            pltpu.VMEM((n_blk, LANE), F32), pltpu.VMEM((n_blk, D_MODEL), F32)]),
        out_shape=jax.ShapeDtypeStruct((b, 1, D_MODEL), BF16),
        compiler_params=_cparams("parallel", "arbitrary"),
        name="moba_dec",
    )(pt, q.reshape(b, 1, D_MODEL), kn.reshape(b, 1, D_MODEL), vn.reshape(b, 1, D_MODEL),
      cache_k, cache_k, cache_v, cache_v, seg, seg_t)
    return out.reshape(b, D_MODEL)


S5_SLAB = 8
S5_SLAB_W = S5_DIM // S5_SLAB


def _s5_in_kernel(x_ref, g_ref, wre_ref, wim_ref, re_ref, im_ref):
    h = _rms(x_ref[...], g_ref[...]).astype(BF16)
    for j in range(S5_SLAB):
        hj = h[:, j * LANE:(j + 1) * LANE]
        cols = slice(j * S5_SLAB_W, (j + 1) * S5_SLAB_W)
        re_ref[:, cols] = _dot(hj, wre_ref[j])
        im_ref[:, cols] = _dot(hj, wim_ref[j])


def _s5_in(x, g, wre, wim):
    m = x.shape[0]
    tm = _row_tile(m)
    spec = pl.BlockSpec((tm, S5_DIM), lambda i: (i, 0))
    return pl.pallas_call(
        _s5_in_kernel,
        grid=(m // tm,),
        in_specs=[pl.BlockSpec((tm, D_MODEL), lambda i: (i, 0)), _resident((1, D_MODEL)),
                  _resident((S5_SLAB, LANE, S5_SLAB_W)), _resident((S5_SLAB, LANE, S5_SLAB_W))],
        out_specs=[spec, spec],
        out_shape=[jax.ShapeDtypeStruct((m, S5_DIM), F32)] * 2,
        compiler_params=_cparams("parallel"),
        name="s5_in",
    )(x, g, wre, wim)


def _s5_scan_kernel(bre_ref, bim_ref, ar_ref, ai_ref, xre_ref, xim_ref, st_ref, *, batch, steps):
    @pl.when(pl.program_id(0) == 0)
    def _():
        st_ref[...] = jnp.zeros_like(st_ref)

    ar = ar_ref[...]
    ai = ai_ref[...]

    def step(t, carry):
        new = []
        for b in range(batch):
            xr, xi = carry[2 * b], carry[2 * b + 1]
            nr = ar * xr - ai * xi + bre_ref[b, t]
            ni = ar * xi + ai * xr + bim_ref[b, t]
            xre_ref[b, t] = nr
            xim_ref[b, t] = ni
            new += [nr, ni]
        return tuple(new)

    init = tuple(st_ref[i] for i in range(2 * batch))
    fin = lax.fori_loop(0, steps, step, init)
    for i in range(2 * batch):
        st_ref[i] = fin[i]


def _s5_scan(bu_re, bu_im, abar_re, abar_im, batch, seq):
    steps = 32
    shape = (batch, seq, S5_SLAB, S5_SLAB_W)
    spec = pl.BlockSpec((batch, steps, S5_SLAB, S5_SLAB_W), lambda t: (0, t, 0, 0))
    kern = functools.partial(_s5_scan_kernel, batch=batch, steps=steps)
    xre, xim = pl.pallas_call(
        kern,
        grid=(seq // steps,),
        in_specs=[spec, spec, _resident((S5_SLAB, S5_SLAB_W)), _resident((S5_SLAB, S5_SLAB_W))],
        out_specs=[spec, spec],
        out_shape=[jax.ShapeDtypeStruct(shape, F32)] * 2,
        scratch_shapes=[pltpu.VMEM((2 * batch, S5_SLAB, S5_SLAB_W), F32)],
        compiler_params=_cparams("arbitrary"),
        name="s5_scan",
    )(bu_re.reshape(shape), bu_im.reshape(shape), abar_re, abar_im)
    return xre.reshape(batch * seq, S5_DIM), xim.reshape(batch * seq, S5_DIM)


def _s5_step_kernel(bre_ref, bim_ref, hr_ref, hi_ref, ar_ref, ai_ref, xre_ref, xim_ref):
    ar, ai, hr, hi = ar_ref[...], ai_ref[...], hr_ref[...], hi_ref[...]
    xre_ref[...] = ar * hr - ai * hi + bre_ref[...]
    xim_ref[...] = ar * hi + ai * hr + bim_ref[...]


def _s5_step(bu_re, bu_im, h_re, h_im, abar_re, abar_im):
    m = bu_re.shape[0]
    full = _resident((m, S5_DIM))
    vec = _resident((1, S5_DIM))
    return pl.pallas_call(
        _s5_step_kernel,
        grid=(1,),
        in_specs=[full, full, full, full, vec, vec],
        out_specs=[pl.BlockSpec((m, S5_DIM), lambda i: (0, 0))] * 2,
        out_shape=[jax.ShapeDtypeStruct((m, S5_DIM), F32)] * 2,
        compiler_params=_cparams("arbitrary"),
        name="s5_step",
    )(bu_re, bu_im, h_re, h_im, abar_re.reshape(1, S5_DIM), abar_im.reshape(1, S5_DIM))


def _s5_out_kernel(x_ref, g_ref, xre_ref, xim_ref, cre_ref, cim_ref, d_ref, wglu_ref, o_ref):
    x = x_ref[...]
    h = _rms(x, g_ref[...])
    ys = []
    for j in range(S5_SLAB):
        cols = slice(j * S5_SLAB_W, (j + 1) * S5_SLAB_W)
        ys.append(_dot(xre_ref[:, cols].astype(BF16), cre_ref[j]) - _dot(xim_ref[:, cols].astype(BF16), cim_ref[j]))
    y = jnp.concatenate(ys, axis=1) + d_ref[...] * h
    gl = _dot(jax.nn.gelu(y).astype(BF16), wglu_ref[...])
    o_ref[...] = x + gl[:, :D_MODEL] * jax.nn.sigmoid(gl[:, D_MODEL:])


def _s5_out(x, g, xre, xim, cre, cim, d, wglu):
    m = x.shape[0]
    tm = _row_tile(m)
    sspec = pl.BlockSpec((tm, S5_DIM), lambda i: (i, 0))
    return pl.pallas_call(
        _s5_out_kernel,
        grid=(m // tm,),
        in_specs=[pl.BlockSpec((tm, D_MODEL), lambda i: (i, 0)), _resident((1, D_MODEL)), sspec, sspec,
                  _resident((S5_SLAB, S5_SLAB_W, LANE)), _resident((S5_SLAB, S5_SLAB_W, LANE)),
                  _resident((1, D_MODEL)), _resident((D_MODEL, 2 * D_MODEL))],
        out_specs=pl.BlockSpec((tm, D_MODEL), lambda i: (i, 0)),
        out_shape=jax.ShapeDtypeStruct((m, D_MODEL), F32),
        compiler_params=_cparams("parallel"),
        name="s5_out",
    )(x, g, xre, xim, cre, cim, d, wglu)


def _s5_params(a_re, a_im, log_dt, b_re, b_im, c_re, c_im):
    dt = jnp.exp(log_dt)[:, None]
    mag = jnp.exp(a_re * dt)
    abar_re = mag * jnp.cos(a_im * dt)
    abar_im = mag * jnp.sin(a_im * dt)
    den = a_re * a_re + a_im * a_im
    f_re = ((abar_re - 1.0) * a_re + abar_im * a_im) / den
    f_im = (abar_im * a_re - (abar_re - 1.0) * a_im) / den
    bbar_re = f_re[..., None] * b_re - f_im[..., None] * b_im
    bbar_im = f_re[..., None] * b_im + f_im[..., None] * b_re
    eye = jnp.eye(S5_SLAB, dtype=F32)

    def in_w(bb):
        bb = bb.reshape(S5_SLAB, S5_SLAB, S5_STATE, S5_GROUP)
        w = jnp.einsum("jgpc,gh->jgchp", bb, eye)
        return w.reshape(S5_SLAB, LANE, S5_SLAB_W).astype(BF16)

    def out_w(cc):
        cc = cc.reshape(S5_SLAB, S5_SLAB, S5_GROUP, S5_STATE)
        w = jnp.einsum("jgcp,gh->jgphc", cc, eye)
        return w.reshape(S5_SLAB, S5_SLAB_W, LANE).astype(BF16)

    return dict(abar_re=abar_re.reshape(S5_SLAB, S5_SLAB_W), abar_im=abar_im.reshape(S5_SLAB, S5_SLAB_W),
                wre=in_w(bbar_re), wim=in_w(bbar_im), cre=out_w(c_re), cim=out_w(c_im))


DIL_TQ = 128


def _dil_kernel(q_ref, kp_ref, kc_ref, vp_ref, vc_ref, o_ref, lse_ref, *, tq):
    t = pl.program_id(3)
    q_all = q_ref[0]
    kcat = jnp.concatenate([kp_ref[0], kc_ref[0]], axis=0).astype(BF16)
    vcat = jnp.concatenate([vp_ref[0], vc_ref[0]], axis=0).astype(BF16)
    lane = lax.broadcasted_iota(jnp.int32, (DIL_TQ, LANE), 1)
    row = lax.broadcasted_iota(jnp.int32, (DIL_TQ, 2 * DIL_TQ), 0)
    col = lax.broadcasted_iota(jnp.int32, (DIL_TQ, 2 * DIL_TQ), 1)
    band = jnp.logical_and(col >= row, col <= row + DIL_TAPS)
    for sub in range(tq // DIL_TQ):
        q = q_all[sub * DIL_TQ:(sub + 1) * DIL_TQ]
        keys = kcat[sub * DIL_TQ:(sub + 2) * DIL_TQ]
        vals = vcat[sub * DIL_TQ:(sub + 2) * DIL_TQ]
        ok = band
        if sub == 0:
            ok = jnp.logical_and(band, jnp.logical_or(t > 0, col >= DIL_TQ))
        o_h, lse_h = [], []
        for hd in range(2):
            mine = (lane < HEAD_DIM) if hd == 0 else (lane >= HEAD_DIM)
            qh = jnp.where(mine, q, jnp.zeros_like(q))
            s = lax.dot_general(qh, keys, _NT, preferred_element_type=F32)
            s = jnp.where(ok, s, -jnp.inf)
            m = jnp.max(s, axis=1, keepdims=True)
            p = jnp.exp(s - m)
            l = jnp.sum(p, axis=1, keepdims=True)
            o_h.append(_dot(p.astype(BF16), vals) / l)
            lse_h.append(m + jnp.log(l))
        rows = slice(sub * DIL_TQ, (sub + 1) * DIL_TQ)
        o_ref[0, rows, :] = jnp.where(lane < HEAD_DIM, o_h[0], o_h[1]).astype(BF16)
        lse_ref[0, rows, :] = jnp.where(lane < HEAD_DIM, lse_h[0], lse_h[1])


def _dil_attn(q, k, v, g, dil, batch, seq):
    sd = seq // dil
    tq = min(512, sd)
    nhp = D_MODEL // LANE
    ncol = N_DIL * nhp
    qv = q.reshape(batch, sd, dil * N_DIL * D_MODEL)
    kv = k.reshape(batch, sd, dil * N_DIL * D_MODEL)
    vv = v.reshape(batch, sd, dil * N_DIL * D_MODEL)
    sub = tq // DIL_TQ
    cur = pl.BlockSpec((1, tq, LANE), lambda b, r, hp, t: (b, t, r * ncol + g * nhp + hp))
    prev = pl.BlockSpec((1, DIL_TQ, LANE),
                        lambda b, r, hp, t: (b, jnp.maximum(t * sub - 1, 0), r * ncol + g * nhp + hp))
    ospec = pl.BlockSpec((1, tq, LANE), lambda b, r, hp, t: (b, t, r * nhp + hp))
    kern = functools.partial(_dil_kernel, tq=tq)
    o, lse = pl.pallas_call(
        kern,
        grid=(batch, dil, nhp, sd // tq),
        in_specs=[cur, prev, cur, prev, cur],
        out_specs=[ospec, ospec],
        out_shape=[jax.ShapeDtypeStruct((batch, sd, dil * D_MODEL), BF16),
                   jax.ShapeDtypeStruct((batch, sd, dil * D_MODEL), F32)],
        compiler_params=_cparams("parallel", "parallel", "parallel", "parallel"),
        name=f"dil_attn{g}",
    )(qv, kv, kv, vv, vv)
    return o.reshape(batch * seq, D_MODEL), lse.reshape(batch * seq, D_MODEL)


def _dil_out_kernel(o0_ref, o1_ref, o2_ref, l0_ref, l1_ref, l2_ref, x_ref, w_ref, out_ref):
    ls = [l0_ref[...], l1_ref[...], l2_ref[...]]
    mx = jnp.maximum(jnp.maximum(ls[0], ls[1]), ls[2])
    es = [jnp.exp(l - mx) for l in ls]
    den = es[0] + es[1] + es[2]
    os_ = [o0_ref[...], o1_ref[...], o2_ref[...]]
    mix = sum((e / den) * o.astype(F32) for e, o in zip(es, os_))
    out_ref[...] = x_ref[...] + _dot(mix.astype(BF16), w_ref[...])


def _dil_out(os_, ls, x, w):
    m = x.shape[0]
    tm = _row_tile(m)
    spec = pl.BlockSpec((tm, D_MODEL), lambda i: (i, 0))
    return pl.pallas_call(
        _dil_out_kernel,
        grid=(m // tm,),
        in_specs=[spec] * 7 + [_resident((D_MODEL, D_MODEL))],
        out_specs=spec,
        out_shape=jax.ShapeDtypeStruct((m, D_MODEL), F32),
        compiler_params=_cparams("parallel"),
        name="dil_out",
    )(*os_, *ls, x, w)


def _dil_dec_kernel(q_ref, kn_ref, vn_ref, k0_ref, v0_ref, k1_ref, v1_ref, k2_ref, v2_ref, seg_ref, segt_ref,
                    o_ref):
    seg, seg_t = seg_ref[...], segt_ref[...]
    caches = ((k0_ref, v0_ref), (k1_ref, v1_ref), (k2_ref, v2_ref))
    ss, sn = [], []
    for g, (kc, _) in enumerate(caches):
        cols = slice(g * D_MODEL, (g + 1) * D_MODEL)
        q = q_ref[0, :, cols].astype(F32)
        ss.append(_split_dot(kc[0] * q, seg))
        sn.append(_split_dot(kn_ref[0, :, cols] * q, seg))
    mx = sn[0]
    for g in range(N_DIL):
        mx = jnp.maximum(mx, jnp.maximum(sn[g], jnp.max(ss[g], axis=0, keepdims=True)))
    den = jnp.zeros((1, LANE), F32)
    num = jnp.zeros((1, D_MODEL), F32)
    for g, (_, vc) in enumerate(caches):
        cols = slice(g * D_MODEL, (g + 1) * D_MODEL)
        p = jnp.exp(ss[g] - mx)
        pn = jnp.exp(sn[g] - mx)
        den = den + jnp.sum(p, axis=0, keepdims=True) + pn
        num = (num + jnp.sum(_dot(p.astype(BF16), seg_t) * vc[0], axis=0, keepdims=True)
               + _split_dot(pn, seg_t) * vn_ref[0, :, cols])
    o_ref[0] = (num / _split_dot(den, seg_t)).astype(BF16)


def _dil_dec(q, kn, vn, caches):
    b = q.shape[0]
    row = pl.BlockSpec((1, 1, N_DIL * D_MODEL), lambda i: (i, 0, 0))
    tap = pl.BlockSpec((1, DIL_TAPS, D_MODEL), lambda i: (i, 0, 0))
    ops = []
    for (kc, vc), (win, dil) in zip(caches, DIL_PAIRS):
        ops += [kc.reshape(b, win // dil, dil * D_MODEL), vc.reshape(b, win // dil, dil * D_MODEL)]
    out = pl.pallas_call(
        _dil_dec_kernel,
        grid=(b,),
        in_specs=[row, row, row] + [tap] * 6 + [_resident((D_MODEL, LANE)), _resident((LANE, D_MODEL))],
        out_specs=pl.BlockSpec((1, 1, D_MODEL), lambda i: (i, 0, 0)),
        out_shape=jax.ShapeDtypeStruct((b, 1, D_MODEL), BF16),
        compiler_params=_cparams("parallel"),
        name="dil_dec",
    )(q.reshape(b, 1, -1), kn.reshape(b, 1, -1), vn.reshape(b, 1, -1), *ops, *_seg_matrices())
    return out.reshape(b, D_MODEL)


def _prep_weights(W):
    P = {}
    row = lambda a: a.reshape(1, -1)
    for i in range(4):
        wup = W["f_w_up"][i].astype(BF16).reshape(D_MODEL, 2 * N_FF_CHUNK, FF_CHUNK).transpose(1, 0, 2)
        wcv = W["f_w_conv"][i].reshape(3, 2 * N_FF_CHUNK, FF_CHUNK).transpose(1, 0, 2)
        wcv = jnp.pad(wcv, ((0, 0), (0, SUBLANE - 3), (0, 0)))
        wdn = W["f_w_down"][i].astype(BF16).reshape(N_FF_CHUNK, FF_CHUNK, D_MODEL)
        P[f"ffn{i}"] = (row(W["norm_ffn"][i]), wup, wcv, wdn, row(W["norm_ple"][i]),
                        W["ple_w_gate"][i].astype(BF16), W["ple_w_proj"][i].astype(BF16), row(W["norm_final"]))
    P["norm_mix"] = [row(W["norm_mix"][i]) for i in range(4)]
    P["a_w_in"] = W["a_w_in"][0].astype(BF16)
    P["a_b_in"] = row(W["a_b_in"][0])
    P["a_w_dw"] = jnp.pad(W["a_w_dw"][0], ((0, CONV_HALO - CONV_A_WIDTH), (0, 0)))
    P["a_b_dw"] = row(W["a_b_dw"][0])
    P["a_ln_g"] = row(W["a_ln_g"][0])
    P["a_ln_b"] = row(W["a_ln_b"][0])
    P["a_w_out"] = W["a_w_out"][0].astype(BF16)
    P["b_w_qkv"] = W["b_w_qkv"][0].astype(BF16)
    P["b_w_o"] = W["b_w_o"][0].astype(BF16)
    P["s5"] = _s5_params(W["c_a_re"][0], W["c_a_im"][0], W["c_log_dt"][0], W["c_b_re"][0], W["c_b_im"][0],
                         W["c_c_re"][0], W["c_c_im"][0])
    P["c_d"] = row(W["c_d"][0])
    P["c_w_glu"] = W["c_w_glu"][0].astype(BF16)
    P["d_w_qkv"] = W["d_w_qkv"][0].astype(BF16)
    P["d_w_o"] = W["d_w_o"][0].astype(BF16)
    return P


def _heads(a, batch, rows):
    return a.reshape(1, batch, rows, N_HEADS, HEAD_DIM)


def _prompt_trunk(x, p, P):
    batch, seq, _ = x.shape
    m = batch * seq
    x = x.reshape(m, D_MODEL)
    tabs = _rope_tables(jnp.arange(seq, dtype=jnp.int32))
    st = {}
    u = _conv_in(x, P["norm_mix"][0], P["a_w_in"], P["a_b_in"])
    st["conv_a"] = u.reshape(batch, seq, D_MODEL)[None, :, seq - (CONV_A_WIDTH - 1):]
    x = _conv_out(u, x, seq, P["a_w_dw"], P["a_b_dw"], P["a_ln_g"], P["a_ln_b"], P["a_w_out"])
    ffn_states = []
    x, fs = _ffn_prompt(x, p[0].reshape(m, PLE_DIM), P["ffn0"], batch, seq, False)
    ffn_states.append(fs)
    q, k, v = _qkv(x, P["norm_mix"][1], P["b_w_qkv"], tabs, 1, seq)
    st["moba_k"] = _heads(k, batch, seq)
    st["moba_v"] = _heads(v, batch, seq)
    o = _moba_attn(q.reshape(batch, seq, D_MODEL), k.reshape(batch, seq, D_MODEL),
                   v.reshape(batch, seq, D_MODEL), batch, seq)
    x = _proj(o.reshape(m, D_MODEL), x, P["b_w_o"])
    x, fs = _ffn_prompt(x, p[1].reshape(m, PLE_DIM), P["ffn1"], batch, seq, False)
    ffn_states.append(fs)
    s5 = P["s5"]
    bu_re, bu_im = _s5_in(x, P["norm_mix"][2], s5["wre"], s5["wim"])
    xre, xim = _s5_scan(bu_re, bu_im, s5["abar_re"], s5["abar_im"], batch, seq)
    st["s5_re"] = xre.reshape(batch, seq, S5_GROUPS, S5_STATE)[None, :, seq - 1]
    st["s5_im"] = xim.reshape(batch, seq, S5_GROUPS, S5_STATE)[None, :, seq - 1]
    x = _s5_out(x, P["norm_mix"][2], xre, xim, s5["cre"], s5["cim"], P["c_d"], P["c_w_glu"])
    x, fs = _ffn_prompt(x, p[2].reshape(m, PLE_DIM), P["ffn2"], batch, seq, False)
    ffn_states.append(fs)
    q, k, v = _qkv(x, P["norm_mix"][3], P["d_w_qkv"], tabs, N_DIL, seq)
    k3 = k.reshape(batch, seq, N_DIL, D_MODEL)
    v3 = v.reshape(batch, seq, N_DIL, D_MODEL)
    os_, ls = [], []
    for g, (win, dil) in enumerate(DIL_PAIRS):
        keep = min(win, seq)
        st[f"win_k{g + 1}"] = _heads(k3[:, seq - keep:, g], batch, keep)
        st[f"win_v{g + 1}"] = _heads(v3[:, seq - keep:, g], batch, keep)
        o, lse = _dil_attn(q.reshape(batch, seq, -1), k.reshape(batch, seq, -1), v.reshape(batch, seq, -1),
                           g, dil, batch, seq)
        os_.append(o)
        ls.append(lse)
    x = _dil_out(os_, ls, x, P["d_w_o"])
    x, fs = _ffn_prompt(x, p[3].reshape(m, PLE_DIM), P["ffn3"], batch, seq, True)
    ffn_states.append(fs)
    st["ffn_conv"] = jnp.stack(ffn_states, 0)
    return x.reshape(batch, seq, D_MODEL), st


def _sample_trunk(x, p, P, S, past_len):
    b = x.shape[0]
    x = x.reshape(b, D_MODEL)
    tabs = _rope_tables(jnp.full((b,), past_len, dtype=jnp.int32))
    st = {}
    buf = S["conv_a"][0]
    x, u = _conv_dec(x, P["norm_mix"][0], P["a_w_in"], P["a_b_in"], buf.transpose(1, 0, 2), P["a_w_dw"],
                     P["a_b_dw"], P["a_ln_g"], P["a_ln_b"], P["a_w_out"])
    st["conv_a"] = jnp.concatenate([buf[:, 1:], u[:, None]], axis=1)[None]
    ffn_states = []
    x, fs = _ffn_dec(x, p[0].reshape(b, PLE_DIM), P["ffn0"], S["ffn_conv"][0], False)
    ffn_states.append(fs)
    q, k, v = _qkv(x, P["norm_mix"][1], P["b_w_qkv"], tabs, 1, b)
    st["moba_k"] = _heads(k, b, 1)
    st["moba_v"] = _heads(v, b, 1)
    n_phys = S["moba_k"].shape[1]
    o = _moba_dec(q, k, v, S["moba_k"][0].reshape(n_phys, PAGE_SIZE, D_MODEL),
                  S["moba_v"][0].reshape(n_phys, PAGE_SIZE, D_MODEL), S["page_table"])
    x = _proj(o, x, P["b_w_o"])
    x, fs = _ffn_dec(x, p[1].reshape(b, PLE_DIM), P["ffn1"], S["ffn_conv"][1], False)
    ffn_states.append(fs)
    s5 = P["s5"]
    bu_re, bu_im = _s5_in(x, P["norm_mix"][2], s5["wre"], s5["wim"])
    xre, xim = _s5_step(bu_re, bu_im, S["s5_re"][0].reshape(b, S5_DIM), S["s5_im"][0].reshape(b, S5_DIM),
                        s5["abar_re"], s5["abar_im"])
    st["s5_re"] = xre.reshape(1, b, S5_GROUPS, S5_STATE)
    st["s5_im"] = xim.reshape(1, b, S5_GROUPS, S5_STATE)
    x = _s5_out(x, P["norm_mix"][2], xre, xim, s5["cre"], s5["cim"], P["c_d"], P["c_w_glu"])
    x, fs = _ffn_dec(x, p[2].reshape(b, PLE_DIM), P["ffn2"], S["ffn_conv"][2], False)
    ffn_states.append(fs)
    q, k, v = _qkv(x, P["norm_mix"][3], P["d_w_qkv"], tabs, N_DIL, b)
    caches = []
    for g, (win, dil) in enumerate(DIL_PAIRS):
        kc, vc = S[f"win_k{g + 1}"][0], S[f"win_v{g + 1}"][0]
        kn = k[:, g * D_MODEL:(g + 1) * D_MODEL].reshape(b, 1, N_HEADS, HEAD_DIM)
        vn = v[:, g * D_MODEL:(g + 1) * D_MODEL].reshape(b, 1, N_HEADS, HEAD_DIM)
        st[f"win_k{g + 1}"] = jnp.concatenate([kc[:, 1:], kn], axis=1)[None]
        st[f"win_v{g + 1}"] = jnp.concatenate([vc[:, 1:], vn], axis=1)[None]
        caches.append((kc.reshape(b, win, D_MODEL), vc.reshape(b, win, D_MODEL)))
    o = _dil_dec(q, k, v, caches)
    x = _proj(o, x, P["d_w_o"])
    x, fs = _ffn_dec(x, p[3].reshape(b, PLE_DIM), P["ffn3"], S["ffn_conv"][3], True)
    ffn_states.append(fs)
    st["ffn_conv"] = jnp.stack(ffn_states, 0)
    return x.reshape(b, 1, D_MODEL), st


_STATE_KEYS = ("conv_a", "moba_k", "moba_v", "s5_re", "s5_im", "win_k1", "win_v1", "win_k2", "win_v2",
               "win_k3", "win_v3", "ffn_conv")


def kernel(x_prompt, x_sample, state_conv_a, cache_moba_k, cache_moba_v, state_s5_re, state_s5_im, cache_win_k1, cache_win_v1, cache_win_k2, cache_win_v2, cache_win_k3, cache_win_v3, state_ffn_conv, page_table, p_prompt, p_sample, norm_mix, norm_ffn, norm_ple, norm_final, a_w_in, a_b_in, a_w_dw, a_b_dw, a_ln_g, a_ln_b, a_w_out, b_w_qkv, b_w_o, c_a_re, c_a_im, c_log_dt, c_b_re, c_b_im, c_c_re, c_c_im, c_d, c_w_glu, d_w_qkv, d_w_o, f_w_up, f_w_conv, f_w_down, ple_w_proj, ple_w_gate):
    W = dict(norm_mix=norm_mix, norm_ffn=norm_ffn, norm_ple=norm_ple, norm_final=norm_final,
             a_w_in=a_w_in, a_b_in=a_b_in, a_w_dw=a_w_dw, a_b_dw=a_b_dw, a_ln_g=a_ln_g, a_ln_b=a_ln_b,
             a_w_out=a_w_out, b_w_qkv=b_w_qkv, b_w_o=b_w_o,
             c_a_re=c_a_re, c_a_im=c_a_im, c_log_dt=c_log_dt, c_b_re=c_b_re, c_b_im=c_b_im,
             c_c_re=c_c_re, c_c_im=c_c_im, c_d=c_d, c_w_glu=c_w_glu,
             d_w_qkv=d_w_qkv, d_w_o=d_w_o, f_w_up=f_w_up, f_w_conv=f_w_conv, f_w_down=f_w_down,
             ple_w_proj=ple_w_proj, ple_w_gate=ple_w_gate)
    S = dict(conv_a=state_conv_a, moba_k=cache_moba_k, moba_v=cache_moba_v, s5_re=state_s5_re, s5_im=state_s5_im,
             win_k1=cache_win_k1, win_v1=cache_win_v1, win_k2=cache_win_k2, win_v2=cache_win_v2,
             win_k3=cache_win_k3, win_v3=cache_win_v3, ffn_conv=state_ffn_conv, page_table=page_table)
    P = _prep_weights(W)
    past_len = page_table.shape[1] * PAGE_SIZE
    y_prompt, sp = _prompt_trunk(x_prompt, p_prompt, P)
    y_sample, ss = _sample_trunk(x_sample, p_sample, P, S, past_len)
    return (y_prompt, y_sample) + tuple(sp[n] for n in _STATE_KEYS) + tuple(ss[n] for n in _STATE_KEYS)
```

```python
import functools
import math

import jax
import jax.numpy as jnp
from jax import lax
from jax.experimental import pallas as pl
from jax.experimental.pallas import tpu as pltpu

F32 = jnp.float32
BF16 = jnp.bfloat16

D_MODEL = 1024
N_HEADS = 16
HEAD_DIM = 64
ROT_DIM = 16
ROPE_THETA = 500000.0
EPS = 1e-6
PLE_DIM = 256
CONV_A_WIDTH = 31
MOBA_BLOCK = 256
MOBA_TOPK = 3
S5_GROUP = 16
S5_GROUPS = 64
S5_STATE = 64
S5_DIM = S5_GROUPS * S5_STATE
DIL_PAIRS = ((128, 1), (512, 4), (2048, 16))
N_DIL = 3
DIL_TAPS = 128
D_FF = 2816
PAGE_SIZE = 128

LANE = 128
SUBLANE = 8
FF_CHUNK = 256
N_FF_CHUNK = D_FF // FF_CHUNK
VMEM_LIMIT = 56 * 1024 * 1024
MASK_NEG = -30000.0

_NT = (((1,), (1,)), ((), ()))


def _cparams(*sem):
    return pltpu.CompilerParams(dimension_semantics=sem, vmem_limit_bytes=VMEM_LIMIT)


def _resident(shape):
    nd = len(shape)
    return pl.BlockSpec(shape, lambda *_: (0,) * nd, pipeline_mode=pl.Buffered(1))


def _rms(x, g):
    return x * lax.rsqrt(jnp.mean(x * x, axis=-1, keepdims=True) + EPS) * g


def _dot(a, b):
    return jnp.dot(a, b, preferred_element_type=F32)


def _row_tile(m):
    return 512 if m % 512 == 0 else m


def _conv_in_kernel(x_ref, g_ref, w_ref, b_ref, u_ref):
    h = _rms(x_ref[...], g_ref[...]).astype(BF16)
    z = _dot(h, w_ref[...]) + b_ref[...]
    u_ref[...] = z[:, :D_MODEL] * jax.nn.sigmoid(z[:, D_MODEL:])


def _conv_in(x, g, w, b):
    m = x.shape[0]
    tm = _row_tile(m)
    return pl.pallas_call(
        _conv_in_kernel,
        grid=(m // tm,),
        in_specs=[pl.BlockSpec((tm, D_MODEL), lambda i: (i, 0)),
                  _resident((1, D_MODEL)), _resident((D_MODEL, 2 * D_MODEL)), _resident((1, 2 * D_MODEL))],
        out_specs=pl.BlockSpec((tm, D_MODEL), lambda i: (i, 0)),
        out_shape=jax.ShapeDtypeStruct((m, D_MODEL), F32),
        compiler_params=_cparams("parallel"),
        name="conv_in",
    )(x, g, w, b)


def _ln_silu_out(cf, x, lng, lnb, wout):
    mu = jnp.mean(cf, axis=-1, keepdims=True)
    var = jnp.mean(jnp.square(cf - mu), axis=-1, keepdims=True)
    cn = (cf - mu) * lax.rsqrt(var + EPS) * lng + lnb
    return x + _dot(jax.nn.silu(cn).astype(BF16), wout)


CONV_HALO = 32


def _conv_out_kernel(u_ref, halo_ref, x_ref, wdw_ref, bdw_ref, lng_ref, lnb_ref, wout_ref, o_ref,
                     ext_ref, cf_ref, *, tm, tiles_per_seq):
    first = (pl.program_id(0) % tiles_per_seq) == 0
    ext_ref[0:CONV_HALO, :] = jnp.where(first, 0.0, halo_ref[...])
    ext_ref[CONV_HALO:, :] = u_ref[...]
    off = CONV_HALO - (CONV_A_WIDTH - 1)
    for c in range(D_MODEL // LANE):
        cols = slice(c * LANE, (c + 1) * LANE)
        acc = jnp.zeros((tm, LANE), F32)
        for j in range(CONV_A_WIDTH):
            acc = acc + wdw_ref[j:j + 1, cols] * ext_ref[pl.ds(off + j, tm), cols]
        cf_ref[:, cols] = acc + bdw_ref[:, cols]
    o_ref[...] = _ln_silu_out(cf_ref[...], x_ref[...], lng_ref[...], lnb_ref[...], wout_ref[...])


def _conv_out(u, x, seq, wdw, bdw, lng, lnb, wout):
    m = x.shape[0]
    tm = 256
    tps = seq // tm
    hb = tm // CONV_HALO
    kern = functools.partial(_conv_out_kernel, tm=tm, tiles_per_seq=tps)
    return pl.pallas_call(
        kern,
        grid=(m // tm,),
        in_specs=[pl.BlockSpec((tm, D_MODEL), lambda i: (i, 0)),
                  pl.BlockSpec((CONV_HALO, D_MODEL), lambda i: (jnp.maximum(i * hb - 1, 0), 0)),
                  pl.BlockSpec((tm, D_MODEL), lambda i: (i, 0)),
                  _resident((CONV_HALO, D_MODEL)), _resident((1, D_MODEL)), _resident((1, D_MODEL)),
                  _resident((1, D_MODEL)), _resident((D_MODEL, D_MODEL))],
        out_specs=pl.BlockSpec((tm, D_MODEL), lambda i: (i, 0)),
        out_shape=jax.ShapeDtypeStruct((m, D_MODEL), F32),
        scratch_shapes=[pltpu.VMEM((tm + CONV_HALO, D_MODEL), F32), pltpu.VMEM((tm, D_MODEL), F32)],
        compiler_params=_cparams("parallel"),
        name="conv_out",
    )(u, u, x, wdw, bdw, lng, lnb, wout)


def _conv_dec_kernel(x_ref, g_ref, win_ref, bin_ref, buf_ref, wdw_ref, bdw_ref, lng_ref, lnb_ref, wout_ref,
                     o_ref, u_ref, acc_ref):
    j = pl.program_id(0)

    @pl.when(j == 0)
    def _():
        h = _rms(x_ref[...], g_ref[...]).astype(BF16)
        z = _dot(h, win_ref[...]) + bin_ref[...]
        u = z[:, :D_MODEL] * jax.nn.sigmoid(z[:, D_MODEL:])
        u_ref[...] = u
        acc_ref[...] = wdw_ref[CONV_A_WIDTH - 1:CONV_A_WIDTH, :] * u

    acc_ref[...] += wdw_ref[pl.ds(j, 1), :] * buf_ref[0]

    @pl.when(j == CONV_A_WIDTH - 2)
    def _():
        cf = acc_ref[...] + bdw_ref[...]
        o_ref[...] = _ln_silu_out(cf, x_ref[...], lng_ref[...], lnb_ref[...], wout_ref[...])


def _conv_dec(x, g, win, bin_, buf_t, wdw, bdw, lng, lnb, wout):
    m = x.shape[0]
    return pl.pallas_call(
        _conv_dec_kernel,
        grid=(CONV_A_WIDTH - 1,),
        in_specs=[_resident((m, D_MODEL)), _resident((1, D_MODEL)), _resident((D_MODEL, 2 * D_MODEL)),
                  _resident((1, 2 * D_MODEL)),
                  pl.BlockSpec((1, m, D_MODEL), lambda j: (j, 0, 0)),
                  _resident((CONV_HALO, D_MODEL)), _resident((1, D_MODEL)), _resident((1, D_MODEL)),
                  _resident((1, D_MODEL)), _resident((D_MODEL, D_MODEL))],
        out_specs=[pl.BlockSpec((m, D_MODEL), lambda j: (0, 0)), pl.BlockSpec((m, D_MODEL), lambda j: (0, 0))],
        out_shape=[jax.ShapeDtypeStruct((m, D_MODEL), F32), jax.ShapeDtypeStruct((m, D_MODEL), F32)],
        scratch_shapes=[pltpu.VMEM((m, D_MODEL), F32)],
        compiler_params=_cparams("arbitrary"),
        name="conv_dec",
    )(x, g, win, bin_, buf_t, wdw, bdw, lng, lnb, wout)


def _ffn_tail(x2, p_ref, gple_ref, wgate_ref, wproj_ref, gfin_ref, final):
    hn = _rms(x2, gple_ref[...]).astype(BF16)
    gate = jax.nn.sigmoid(_dot(hn, wgate_ref[...]))
    x3 = x2 + _dot(p_ref[...].astype(BF16), wproj_ref[...]) * gate
    if final:
        x3 = _rms(x3, gfin_ref[...])
    return x3


def _ffn_act_acc(ca, cb, j, wdn_ref, acc_ref):
    act = (jax.nn.silu(ca) * cb).astype(BF16)
    acc_ref[...] += _dot(act, wdn_ref[j])


def _ffn_prompt_kernel(x_ref, p_ref, gffn_ref, wup_ref, wcv_ref, wdn_ref, gple_ref, wgate_ref, wproj_ref,
                       gfin_ref, o_ref, st_ref, h_ref, acc_ref, ua_ref, ub_ref, carry_ref,
                       *, tm, tiles_per_seq, final):
    t = pl.program_id(0) % tiles_per_seq

    @pl.when(t == 0)
    def _():
        carry_ref[...] = jnp.zeros_like(carry_ref)

    h_ref[...] = _rms(x_ref[...], gffn_ref[...]).astype(BF16)
    acc_ref[...] = jnp.zeros_like(acc_ref)

    def conv_half(c, ubuf):
        u = _dot(h_ref[...], wup_ref[c])
        ubuf[0:SUBLANE, :] = carry_ref[c]
        ubuf[SUBLANE:, :] = u
        w = wcv_ref[c]
        cc = (w[0:1] * ubuf[pl.ds(SUBLANE - 2, tm), :] + w[1:2] * ubuf[pl.ds(SUBLANE - 1, tm), :]
              + w[2:3] * u)
        carry_ref[c] = ubuf[pl.ds(tm, SUBLANE), :]
        return cc

    def chunk(j, carry):
        ca = conv_half(j, ua_ref)
        cb = conv_half(j + N_FF_CHUNK, ub_ref)
        _ffn_act_acc(ca, cb, j, wdn_ref, acc_ref)
        return carry

    lax.fori_loop(0, N_FF_CHUNK, chunk, 0)

    @pl.when(t == tiles_per_seq - 1)
    def _():
        st_ref[0] = carry_ref[...]

    x2 = x_ref[...] + acc_ref[...]
    o_ref[...] = _ffn_tail(x2, p_ref, gple_ref, wgate_ref, wproj_ref, gfin_ref, final)


def _ffn_dec_kernel(x_ref, p_ref, gffn_ref, wup_ref, wcv_ref, wdn_ref, gple_ref, wgate_ref, wproj_ref,
                    gfin_ref, s0_ref, s1_ref, o_ref, u_ref, h_ref, acc_ref, *, final):
    h_ref[...] = _rms(x_ref[...], gffn_ref[...]).astype(BF16)
    acc_ref[...] = jnp.zeros_like(acc_ref)

    def conv_half(c):
        u = _dot(h_ref[...], wup_ref[c])
        u_ref[c] = u
        w = wcv_ref[c]
        return w[0:1] * s0_ref[c] + w[1:2] * s1_ref[c] + w[2:3] * u

    def chunk(j, carry):
        ca = conv_half(j)
        cb = conv_half(j + N_FF_CHUNK)
        _ffn_act_acc(ca, cb, j, wdn_ref, acc_ref)
        return carry

    lax.fori_loop(0, N_FF_CHUNK, chunk, 0)
    x2 = x_ref[...] + acc_ref[...]
    o_ref[...] = _ffn_tail(x2, p_ref, gple_ref, wgate_ref, wproj_ref, gfin_ref, final)


def _ffn_weight_specs():
    return [_resident((1, D_MODEL)),
            _resident((2 * N_FF_CHUNK, D_MODEL, FF_CHUNK)),
            _resident((2 * N_FF_CHUNK, SUBLANE, FF_CHUNK)),
            _resident((N_FF_CHUNK, FF_CHUNK, D_MODEL)),
            _resident((1, D_MODEL)), _resident((D_MODEL, D_MODEL)), _resident((PLE_DIM, D_MODEL)),
            _resident((1, D_MODEL))]


def _ffn_prompt(x, p, wts, batch, seq, final):
    m = x.shape[0]
    tm = _row_tile(seq)
    tps = seq // tm
    kern = functools.partial(_ffn_prompt_kernel, tm=tm, tiles_per_seq=tps, final=final)
    nc = 2 * N_FF_CHUNK
    out, st = pl.pallas_call(
        kern,
        grid=(m // tm,),
        in_specs=[pl.BlockSpec((tm, D_MODEL), lambda i: (i, 0)),
                  pl.BlockSpec((tm, PLE_DIM), lambda i: (i, 0))] + _ffn_weight_specs(),
        out_specs=[pl.BlockSpec((tm, D_MODEL), lambda i: (i, 0)),
                   pl.BlockSpec((1, nc, SUBLANE, FF_CHUNK), lambda i: (i // tps, 0, 0, 0))],
        out_shape=[jax.ShapeDtypeStruct((m, D_MODEL), F32),
                   jax.ShapeDtypeStruct((batch, nc, SUBLANE, FF_CHUNK), F32)],
        scratch_shapes=[pltpu.VMEM((tm, D_MODEL), BF16), pltpu.VMEM((tm, D_MODEL), F32),
                        pltpu.VMEM((tm + SUBLANE, FF_CHUNK), F32), pltpu.VMEM((tm + SUBLANE, FF_CHUNK), F32),
                        pltpu.VMEM((nc, SUBLANE, FF_CHUNK), F32)],
        compiler_params=_cparams("arbitrary"),
        name="ffn_prompt",
    )(x, p, *wts)
    new_state = st[:, :, SUBLANE - 2:, :].transpose(0, 2, 1, 3).reshape(batch, 2, 2 * D_FF)
    return out, new_state


def _ffn_dec(x, p, wts, state, final):
    m = x.shape[0]
    nc = 2 * N_FF_CHUNK
    kern = functools.partial(_ffn_dec_kernel, final=final)
    s = state.reshape(m, 2, nc, FF_CHUNK).transpose(1, 2, 0, 3)
    out, u = pl.pallas_call(
        kern,
        grid=(1,),
        in_specs=[_resident((m, D_MODEL)), _resident((m, PLE_DIM))] + _ffn_weight_specs()
        + [_resident((nc, m, FF_CHUNK)), _resident((nc, m, FF_CHUNK))],
        out_specs=[pl.BlockSpec((m, D_MODEL), lambda i: (0, 0)),
                   pl.BlockSpec((nc, m, FF_CHUNK), lambda i: (0, 0, 0))],
        out_shape=[jax.ShapeDtypeStruct((m, D_MODEL), F32), jax.ShapeDtypeStruct((nc, m, FF_CHUNK), F32)],
        scratch_shapes=[pltpu.VMEM((m, D_MODEL), BF16), pltpu.VMEM((m, D_MODEL), F32)],
        compiler_params=_cparams("arbitrary"),
        name="ffn_dec",
    )(x, p, *wts, s[0], s[1])
    u_rows = u.transpose(1, 0, 2).reshape(m, 1, 2 * D_FF)
    return out, jnp.concatenate([state[:, 1:], u_rows], axis=1)


def _qkv_kernel(x_ref, g_ref, w_ref, c_ref, s1_ref, s2_ref, q_ref, k_ref, v_ref, *t_refs):
    h = _rms(x_ref[...], g_ref[...]).astype(BF16)
    reps = D_MODEL // LANE
    cos = jnp.concatenate([c_ref[...]] * reps, axis=1)
    sn1 = jnp.concatenate([s1_ref[...]] * reps, axis=1)
    sn2 = jnp.concatenate([s2_ref[...]] * reps, axis=1)
    half = ROT_DIM // 2

    def rope(z):
        return (z * cos + pltpu.roll(z, D_MODEL - half, 1) * sn1 + pltpu.roll(z, half, 1) * sn2)

    q = rope(_dot(h, w_ref[:, 0:D_MODEL]))
    q_ref[...] = (q * (HEAD_DIM ** -0.5)).astype(BF16)
    k = rope(_dot(h, w_ref[:, D_MODEL:2 * D_MODEL]))
    v = _dot(h, w_ref[:, 2 * D_MODEL:])
    k_ref[...] = k.astype(k_ref.dtype)
    v_ref[...] = v.astype(v_ref.dtype)
    if t_refs:
        kt_ref, vt_ref = t_refs
        kt_ref[0] = k.T
        vt_ref[0] = v.T


def _qkv(x, g, w, tabs, n_groups, seq, batch=None):
    m = x.shape[0]
    tm = _row_tile(seq)
    npb = seq // tm
    row = lambda gi, i: (i, 0)
    col = lambda gi, i: (i, gi)
    tab = pl.BlockSpec((tm, LANE), lambda gi, i: (i % npb, 0))
    n = n_groups * D_MODEL
    kv_dtype = F32 if batch is None else BF16
    out_specs = [pl.BlockSpec((tm, D_MODEL), col)] * 3
    out_shape = [jax.ShapeDtypeStruct((m, n), BF16), jax.ShapeDtypeStruct((m, n), kv_dtype),
                 jax.ShapeDtypeStruct((m, n), kv_dtype)]
    if batch is not None:
        out_specs += [pl.BlockSpec((1, D_MODEL, tm), lambda gi, i: (i // npb, gi, i % npb))] * 2
        out_shape += [jax.ShapeDtypeStruct((batch, n, seq), F32)] * 2
    return pl.pallas_call(
        _qkv_kernel,
        grid=(n_groups, m // tm),
        in_specs=[pl.BlockSpec((tm, D_MODEL), row), pl.BlockSpec((1, D_MODEL), lambda gi, i: (0, 0)),
                  pl.BlockSpec((D_MODEL, 3 * D_MODEL), lambda gi, i: (0, gi)), tab, tab, tab],
        out_specs=out_specs,
        out_shape=out_shape,
        compiler_params=_cparams("parallel", "parallel"),
        name="qkv_rope",
    )(x, g, w, *tabs)


def _rope_tables(pos):
    half = ROT_DIM // 2
    inv_freq = ROPE_THETA ** (-jnp.arange(half, dtype=F32) / half)
    ang = pos.astype(F32)[:, None] * inv_freq[None, :]
    cos, sin = jnp.cos(ang), jnp.sin(ang)
    lane = jnp.arange(LANE) % HEAD_DIM
    idx = lane % half
    c = jnp.where(lane[None, :] < ROT_DIM, cos[:, idx], 1.0)
    s1 = jnp.where(lane[None, :] < half, -sin[:, idx], 0.0)
    s2 = jnp.where((lane[None, :] >= half) & (lane[None, :] < ROT_DIM), sin[:, idx], 0.0)
    return c, s1, s2


def _proj_kernel(a_ref, x_ref, w_ref, o_ref):
    o_ref[...] = x_ref[...] + _dot(a_ref[...], w_ref[...])


def _proj(a, x, w):
    m = x.shape[0]
    tm = _row_tile(m)
    return pl.pallas_call(
        _proj_kernel,
        grid=(m // tm,),
        in_specs=[pl.BlockSpec((tm, D_MODEL), lambda i: (i, 0)), pl.BlockSpec((tm, D_MODEL), lambda i: (i, 0)),
                  _resident((D_MODEL, D_MODEL))],
        out_specs=pl.BlockSpec((tm, D_MODEL), lambda i: (i, 0)),
        out_shape=jax.ShapeDtypeStruct((m, D_MODEL), F32),
        compiler_params=_cparams("parallel"),
        name="out_proj",
    )(a, x, w)


def _moba_kernel(q_ref, k_ref, v_ref, o_ref, ka0_ref, ka1_ref, km_ref, *, seq):
    nb = seq // MOBA_BLOCK
    qi = pl.program_id(2)
    blk = MOBA_BLOCK
    span = 2 * blk

    @pl.when(qi == 0)
    def _():
        kf = k_ref[0].astype(F32)
        rblk = lax.broadcasted_iota(jnp.int32, (seq, LANE), 0) >> int(math.log2(blk))
        lane = lax.broadcasted_iota(jnp.int32, (seq, LANE), 1)
        ka0_ref[...] = jnp.where(lane < HEAD_DIM, kf, (lane - HEAD_DIM == rblk).astype(F32)).astype(BF16)
        ka1_ref[...] = jnp.where(lane >= HEAD_DIM, kf, (lane == rblk).astype(F32)).astype(BF16)
        km_ref[...] = jnp.mean(kf.reshape(nb, blk, LANE), axis=1)

    q2 = q_ref[0]
    lane_q = lax.broadcasted_iota(jnp.int32, (blk, LANE), 1)
    lane_k = lax.broadcasted_iota(jnp.int32, (nb, LANE), 1)
    n_iota = lax.broadcasted_iota(jnp.int32, (nb, blk), 0)
    row = lax.broadcasted_iota(jnp.int32, (blk, blk), 0)
    colk = lax.broadcasted_iota(jnp.int32, (blk, blk), 1)
    km = km_ref[...]
    start = pl.multiple_of(qi * blk, blk)
    v_own = v_ref[0, pl.ds(start, blk), :]
    qas, init = [], []
    for hd, ka_ref in ((0, ka0_ref), (1, ka1_ref)):
        mine_q = (lane_q < HEAD_DIM) if hd == 0 else (lane_q >= HEAD_DIM)
        mine_k = (lane_k < HEAD_DIM) if hd == 0 else (lane_k >= HEAD_DIM)
        kmh = jnp.where(mine_k, km, 0.0)
        hi = kmh.astype(BF16)
        lo = (kmh - hi.astype(F32)).astype(BF16)
        st = (lax.dot_general(hi, q2, _NT, preferred_element_type=F32)
              + lax.dot_general(lo, q2, _NT, preferred_element_type=F32))
        s = jnp.where(n_iota < qi, st, -jnp.inf)
        pen = jnp.full((nb, blk), MASK_NEG, F32)
        for _ in range(MOBA_TOPK):
            mx = jnp.max(s, axis=0, keepdims=True)
            idx = jnp.min(jnp.where(s == mx, n_iota, nb), axis=0, keepdims=True)
            chosen = jnp.logical_and(n_iota == idx, mx > -jnp.inf)
            pen = jnp.where(chosen, 0.0, pen)
            s = jnp.where(chosen, -jnp.inf, s)
        n_lo = HEAD_DIM if hd == 0 else 0
        parts = [jnp.zeros((n_lo, blk), F32)] if n_lo else []
        parts += [pen, jnp.zeros((LANE - nb - n_lo, blk), F32)]
        pen_q = jnp.concatenate(parts, axis=0).T
        qas.append(jnp.where(mine_q, q2, pen_q.astype(BF16)))

        qd = jnp.where(mine_q, q2, jnp.zeros_like(q2))
        s0 = lax.dot_general(qd, ka_ref[pl.ds(start, blk), :], _NT, preferred_element_type=F32)
        s0 = jnp.where(colk <= row, s0, -jnp.inf)
        m0 = jnp.max(s0, axis=1, keepdims=True)
        p0 = jnp.exp(s0 - m0)
        init += [m0, jnp.sum(p0, axis=1, keepdims=True), _dot(p0.astype(BF16), v_own)]

    def past(j, carry):
        off = pl.multiple_of(j * span, span)
        vj = v_ref[0, pl.ds(off, span), :]
        new = []
        for hd, ka_ref in ((0, ka0_ref), (1, ka1_ref)):
            m, l, acc = carry[3 * hd:3 * hd + 3]
            sn = lax.dot_general(qas[hd], ka_ref[pl.ds(off, span), :], _NT, preferred_element_type=F32)
            mn = jnp.maximum(m, jnp.max(sn, axis=1, keepdims=True))
            alpha = jnp.exp(m - mn)
            p = jnp.exp(sn - mn)
            new += [mn, alpha * l + jnp.sum(p, axis=1, keepdims=True), alpha * acc + _dot(p.astype(BF16), vj)]
        return tuple(new)

    fin = lax.fori_loop(0, (qi + 1) // 2, past, tuple(init))
    o_ref[0] = jnp.where(lane_q < HEAD_DIM, fin[2] / fin[1], fin[5] / fin[4]).astype(BF16)


def _moba_attn(q, k, v, batch, seq):
    nhp = D_MODEL // LANE
    nq = seq // MOBA_BLOCK
    kern = functools.partial(_moba_kernel, seq=seq)
    qspec = pl.BlockSpec((1, MOBA_BLOCK, LANE), lambda b, hp, qi: (b, qi, hp))
    kvspec = pl.BlockSpec((1, seq, LANE), lambda b, hp, qi: (b, 0, hp))
    return pl.pallas_call(
        kern,
        grid=(batch, nhp, nq),
        in_specs=[qspec, kvspec, kvspec],
        out_specs=qspec,
        out_shape=jax.ShapeDtypeStruct((batch, seq, D_MODEL), BF16),
        scratch_shapes=[pltpu.VMEM((seq, LANE), BF16), pltpu.VMEM((seq, LANE), BF16),
                        pltpu.VMEM((seq // MOBA_BLOCK, LANE), F32)],
        compiler_params=_cparams("parallel", "parallel", "arbitrary"),
        name="moba_attn",
    )(q, k, v)


def _batch_column(ref, b, rows=None):
    tile = ref[...] if rows is None else ref[rows, :]
    lane = lax.broadcasted_iota(jnp.int32, tile.shape, 1)
    return jnp.sum(jnp.where(lane == b, tile, 0.0), axis=1, keepdims=True)


def _moba_dec_kernel(pt_ref, qt_ref, knt_ref, vnt_ref, k0_ref, k1_ref, v0_ref, v1_ref, o_ref,
                     qb_ref, s_ref, vall_ref, *, n_blk):
    b, n = pl.program_id(0), pl.program_id(1)
    hshape = (N_HEADS, HEAD_DIM, PAGE_SIZE)

    @pl.when(n == 0)
    def _():
        qb_ref[...] = jnp.broadcast_to(_batch_column(qt_ref, b), (D_MODEL, PAGE_SIZE))

    qb = qb_ref[...].reshape(hshape)
    for j, (kr, vr) in enumerate(((k0_ref, v0_ref), (k1_ref, v1_ref))):
        s_ref[2 * n + j] = jnp.sum(kr[0].reshape(hshape) * qb, axis=1, keepdims=True)
        vall_ref[2 * n + j] = vr[0]

    @pl.when(n == n_blk - 1)
    def _():
        cshape = (N_HEADS, HEAD_DIM, 1)
        qc = qb_ref[:, 0:1].reshape(cshape)
        knc = _batch_column(knt_ref, b).reshape(cshape)
        vnc = _batch_column(vnt_ref, b).reshape(cshape)
        s_own = jnp.sum(qc * knc, axis=1, keepdims=True)
        bsum, bmax = [], []
        for i in range(n_blk):
            a, c = s_ref[2 * i], s_ref[2 * i + 1]
            bsum.append(jnp.sum(a + c, axis=2, keepdims=True))
            bmax.append(jnp.max(jnp.maximum(a, c), axis=2, keepdims=True))
        sels = []
        for i in range(n_blk):
            rank = jnp.zeros((N_HEADS, 1, 1), F32)
            for m in range(n_blk):
                if m != i:
                    beats = (bsum[m] >= bsum[i]) if m < i else (bsum[m] > bsum[i])
                    rank = rank + jnp.where(beats, 1.0, 0.0)
            sels.append(rank < MOBA_TOPK)
        mx = s_own
        for i in range(n_blk):
            mx = jnp.maximum(mx, jnp.where(sels[i], bmax[i], -jnp.inf))
        w_own = jnp.exp(s_own - mx)
        lacc = jnp.zeros((N_HEADS, 1, PAGE_SIZE), F32)
        vacc = jnp.zeros(hshape, F32)
        for pg in range(2 * n_blk):
            p = jnp.where(sels[pg // 2], jnp.exp(s_ref[pg] - mx), 0.0)
            lacc = lacc + p
            vacc = vacc + p * vall_ref[pg].reshape(hshape)
        den = w_own + jnp.sum(lacc, axis=2, keepdims=True)
        o = (w_own * vnc + jnp.sum(vacc, axis=2, keepdims=True)) / den
        o_ref[0] = jnp.broadcast_to(o, hshape).reshape(D_MODEL, PAGE_SIZE)


def _moba_dec(qt, knt, vnt, cache_k, cache_v, page_table):
    b, n_pages = page_table.shape
    ppb = MOBA_BLOCK // PAGE_SIZE
    n_blk = n_pages // ppb
    pt = page_table.reshape(-1)
    vec = pl.BlockSpec((D_MODEL, b), lambda i, n, pt: (0, 0))

    def page(off):
        return pl.BlockSpec((1, D_MODEL, PAGE_SIZE), lambda i, n, pt: (pt[i * n_pages + n * ppb + off], 0, 0))

    kern = functools.partial(_moba_dec_kernel, n_blk=n_blk)
    return pl.pallas_call(
        kern,
        grid_spec=pltpu.PrefetchScalarGridSpec(
            num_scalar_prefetch=1,
            grid=(b, n_blk),
            in_specs=[vec, vec, vec, page(0), page(1), page(0), page(1)],
            out_specs=pl.BlockSpec((1, D_MODEL, PAGE_SIZE), lambda i, n, pt: (i, 0, 0)),
            scratch_shapes=[pltpu.VMEM((D_MODEL, PAGE_SIZE), F32),
                            pltpu.VMEM((n_pages, N_HEADS, 1, PAGE_SIZE), F32),
                            pltpu.VMEM((n_pages, D_MODEL, PAGE_SIZE), F32)]),
        out_shape=jax.ShapeDtypeStruct((b, D_MODEL, PAGE_SIZE), F32),
        compiler_params=_cparams("parallel", "arbitrary"),
        name="moba_dec",
    )(pt, qt, knt, vnt, cache_k, cache_k, cache_v, cache_v)


S5_SLAB = 8
S5_SLAB_W = S5_DIM // S5_SLAB


def _s5_in_kernel(x_ref, g_ref, wre_ref, wim_ref, re_ref, im_ref):
    h = _rms(x_ref[...], g_ref[...]).astype(BF16)
    for j in range(S5_SLAB):
        hj = h[:, j * LANE:(j + 1) * LANE]
        cols = slice(j * S5_SLAB_W, (j + 1) * S5_SLAB_W)
        re_ref[:, cols] = _dot(hj, wre_ref[j])
        im_ref[:, cols] = _dot(hj, wim_ref[j])


def _s5_in(x, g, wre, wim):
    m = x.shape[0]
    tm = _row_tile(m)
    spec = pl.BlockSpec((tm, S5_DIM), lambda i: (i, 0))
    return pl.pallas_call(
        _s5_in_kernel,
        grid=(m // tm,),
        in_specs=[pl.BlockSpec((tm, D_MODEL), lambda i: (i, 0)), _resident((1, D_MODEL)),
                  _resident((S5_SLAB, LANE, S5_SLAB_W)), _resident((S5_SLAB, LANE, S5_SLAB_W))],
        out_specs=[spec, spec],
        out_shape=[jax.ShapeDtypeStruct((m, S5_DIM), F32)] * 2,
        compiler_params=_cparams("parallel"),
        name="s5_in",
    )(x, g, wre, wim)


def _s5_scan_kernel(bre_ref, bim_ref, ar_ref, ai_ref, xre_ref, xim_ref, st_ref, *, batch, steps):
    @pl.when(pl.program_id(0) == 0)
    def _():
        st_ref[...] = jnp.zeros_like(st_ref)

    ar = ar_ref[...]
    ai = ai_ref[...]

    def step(t, carry):
        new = []
        for b in range(batch):
            xr, xi = carry[2 * b], carry[2 * b + 1]
            nr = ar * xr - ai * xi + bre_ref[b, t]
            ni = ar * xi + ai * xr + bim_ref[b, t]
            xre_ref[b, t] = nr
            xim_ref[b, t] = ni
            new += [nr, ni]
        return tuple(new)

    init = tuple(st_ref[i] for i in range(2 * batch))
    fin = lax.fori_loop(0, steps, step, init)
    for i in range(2 * batch):
        st_ref[i] = fin[i]


def _s5_scan(bu_re, bu_im, abar_re, abar_im, batch, seq):
    steps = 32
    shape = (batch, seq, S5_SLAB, S5_SLAB_W)
    spec = pl.BlockSpec((batch, steps, S5_SLAB, S5_SLAB_W), lambda t: (0, t, 0, 0))
    kern = functools.partial(_s5_scan_kernel, batch=batch, steps=steps)
    xre, xim = pl.pallas_call(
        kern,
        grid=(seq // steps,),
        in_specs=[spec, spec, _resident((S5_SLAB, S5_SLAB_W)), _resident((S5_SLAB, S5_SLAB_W))],
        out_specs=[spec, spec],
        out_shape=[jax.ShapeDtypeStruct(shape, F32)] * 2,
        scratch_shapes=[pltpu.VMEM((2 * batch, S5_SLAB, S5_SLAB_W), F32)],
        compiler_params=_cparams("arbitrary"),
        name="s5_scan",
    )(bu_re.reshape(shape), bu_im.reshape(shape), abar_re, abar_im)
    return xre.reshape(batch * seq, S5_DIM), xim.reshape(batch * seq, S5_DIM)


def _s5_step_kernel(bre_ref, bim_ref, hr_ref, hi_ref, ar_ref, ai_ref, xre_ref, xim_ref):
    ar, ai, hr, hi = ar_ref[...], ai_ref[...], hr_ref[...], hi_ref[...]
    xre_ref[...] = ar * hr - ai * hi + bre_ref[...]
    xim_ref[...] = ar * hi + ai * hr + bim_ref[...]


def _s5_step(bu_re, bu_im, h_re, h_im, abar_re, abar_im):
    m = bu_re.shape[0]
    full = _resident((m, S5_DIM))
    vec = _resident((1, S5_DIM))
    return pl.pallas_call(
        _s5_step_kernel,
        grid=(1,),
        in_specs=[full, full, full, full, vec, vec],
        out_specs=[pl.BlockSpec((m, S5_DIM), lambda i: (0, 0))] * 2,
        out_shape=[jax.ShapeDtypeStruct((m, S5_DIM), F32)] * 2,
        compiler_params=_cparams("arbitrary"),
        name="s5_step",
    )(bu_re, bu_im, h_re, h_im, abar_re.reshape(1, S5_DIM), abar_im.reshape(1, S5_DIM))


def _s5_out_kernel(x_ref, g_ref, xre_ref, xim_ref, cre_ref, cim_ref, d_ref, wglu_ref, o_ref):
    x = x_ref[...]
    h = _rms(x, g_ref[...])
    ys = []
    for j in range(S5_SLAB):
        cols = slice(j * S5_SLAB_W, (j + 1) * S5_SLAB_W)
        ys.append(_dot(xre_ref[:, cols].astype(BF16), cre_ref[j]) - _dot(xim_ref[:, cols].astype(BF16), cim_ref[j]))
    y = jnp.concatenate(ys, axis=1) + d_ref[...] * h
    gl = _dot(jax.nn.gelu(y).astype(BF16), wglu_ref[...])
    o_ref[...] = x + gl[:, :D_MODEL] * jax.nn.sigmoid(gl[:, D_MODEL:])


def _s5_out(x, g, xre, xim, cre, cim, d, wglu):
    m = x.shape[0]
    tm = _row_tile(m)
    sspec = pl.BlockSpec((tm, S5_DIM), lambda i: (i, 0))
    return pl.pallas_call(
        _s5_out_kernel,
        grid=(m // tm,),
        in_specs=[pl.BlockSpec((tm, D_MODEL), lambda i: (i, 0)), _resident((1, D_MODEL)), sspec, sspec,
                  _resident((S5_SLAB, S5_SLAB_W, LANE)), _resident((S5_SLAB, S5_SLAB_W, LANE)),
                  _resident((1, D_MODEL)), _resident((D_MODEL, 2 * D_MODEL))],
        out_specs=pl.BlockSpec((tm, D_MODEL), lambda i: (i, 0)),
        out_shape=jax.ShapeDtypeStruct((m, D_MODEL), F32),
        compiler_params=_cparams("parallel"),
        name="s5_out",
    )(x, g, xre, xim, cre, cim, d, wglu)


def _s5_params(a_re, a_im, log_dt, b_re, b_im, c_re, c_im):
    dt = jnp.exp(log_dt)[:, None]
    mag = jnp.exp(a_re * dt)
    abar_re = mag * jnp.cos(a_im * dt)
    abar_im = mag * jnp.sin(a_im * dt)
    den = a_re * a_re + a_im * a_im
    f_re = ((abar_re - 1.0) * a_re + abar_im * a_im) / den
    f_im = (abar_im * a_re - (abar_re - 1.0) * a_im) / den
    bbar_re = f_re[..., None] * b_re - f_im[..., None] * b_im
    bbar_im = f_re[..., None] * b_im + f_im[..., None] * b_re
    eye = jnp.eye(S5_SLAB, dtype=F32)

    def in_w(bb):
        bb = bb.reshape(S5_SLAB, S5_SLAB, S5_STATE, S5_GROUP)
        w = jnp.einsum("jgpc,gh->jgchp", bb, eye)
        return w.reshape(S5_SLAB, LANE, S5_SLAB_W).astype(BF16)

    def out_w(cc):
        cc = cc.reshape(S5_SLAB, S5_SLAB, S5_GROUP, S5_STATE)
        w = jnp.einsum("jgcp,gh->jgphc", cc, eye)
        return w.reshape(S5_SLAB, S5_SLAB_W, LANE).astype(BF16)

    return dict(abar_re=abar_re.reshape(S5_SLAB, S5_SLAB_W), abar_im=abar_im.reshape(S5_SLAB, S5_SLAB_W),
                wre=in_w(bbar_re), wim=in_w(bbar_im), cre=out_w(c_re), cim=out_w(c_im))


DIL_TQ = 128


def _dil_kernel(q_ref, kp_ref, kc_ref, vp_ref, vc_ref, o_ref, lse_ref, *, tq):
    t = pl.program_id(3)
    q_all = q_ref[0]
    kcat = jnp.concatenate([kp_ref[0], kc_ref[0]], axis=0).astype(BF16)
    vcat = jnp.concatenate([vp_ref[0], vc_ref[0]], axis=0).astype(BF16)
    lane = lax.broadcasted_iota(jnp.int32, (DIL_TQ, LANE), 1)
    row = lax.broadcasted_iota(jnp.int32, (DIL_TQ, 2 * DIL_TQ), 0)
    col = lax.broadcasted_iota(jnp.int32, (DIL_TQ, 2 * DIL_TQ), 1)
    band = jnp.logical_and(col >= row, col <= row + DIL_TAPS)
    for sub in range(tq // DIL_TQ):
        q = q_all[sub * DIL_TQ:(sub + 1) * DIL_TQ]
        keys = kcat[sub * DIL_TQ:(sub + 2) * DIL_TQ]
        vals = vcat[sub * DIL_TQ:(sub + 2) * DIL_TQ]
        ok = band
        if sub == 0:
            ok = jnp.logical_and(band, jnp.logical_or(t > 0, col >= DIL_TQ))
        o_h, lse_h = [], []
        for hd in range(2):
            mine = (lane < HEAD_DIM) if hd == 0 else (lane >= HEAD_DIM)
            qh = jnp.where(mine, q, jnp.zeros_like(q))
            s = lax.dot_general(qh, keys, _NT, preferred_element_type=F32)
            s = jnp.where(ok, s, -jnp.inf)
            m = jnp.max(s, axis=1, keepdims=True)
            p = jnp.exp(s - m)
            l = jnp.sum(p, axis=1, keepdims=True)
            o_h.append(_dot(p.astype(BF16), vals) / l)
            lse_h.append(m + jnp.log(l))
        rows = slice(sub * DIL_TQ, (sub + 1) * DIL_TQ)
        o_ref[0, rows, :] = jnp.where(lane < HEAD_DIM, o_h[0], o_h[1]).astype(BF16)
        lse_ref[0, rows, :] = jnp.where(lane < HEAD_DIM, lse_h[0], lse_h[1])


def _dil_attn(q, k, v, g, dil, batch, seq):
    sd = seq // dil
    tq = min(512, sd)
    nhp = D_MODEL // LANE
    ncol = N_DIL * nhp
    qv = q.reshape(batch, sd, dil * N_DIL * D_MODEL)
    kv = k.reshape(batch, sd, dil * N_DIL * D_MODEL)
    vv = v.reshape(batch, sd, dil * N_DIL * D_MODEL)
    sub = tq // DIL_TQ
    cur = pl.BlockSpec((1, tq, LANE), lambda b, r, hp, t: (b, t, r * ncol + g * nhp + hp))
    prev = pl.BlockSpec((1, DIL_TQ, LANE),
                        lambda b, r, hp, t: (b, jnp.maximum(t * sub - 1, 0), r * ncol + g * nhp + hp))
    ospec = pl.BlockSpec((1, tq, LANE), lambda b, r, hp, t: (b, t, r * nhp + hp))
    kern = functools.partial(_dil_kernel, tq=tq)
    o, lse = pl.pallas_call(
        kern,
        grid=(batch, dil, nhp, sd // tq),
        in_specs=[cur, prev, cur, prev, cur],
        out_specs=[ospec, ospec],
        out_shape=[jax.ShapeDtypeStruct((batch, sd, dil * D_MODEL), BF16),
                   jax.ShapeDtypeStruct((batch, sd, dil * D_MODEL), F32)],
        compiler_params=_cparams("parallel", "parallel", "parallel", "parallel"),
        name=f"dil_attn{g}",
    )(qv, kv, kv, vv, vv)
    return o.reshape(batch * seq, D_MODEL), lse.reshape(batch * seq, D_MODEL)


def _dil_out_kernel(o0_ref, o1_ref, o2_ref, l0_ref, l1_ref, l2_ref, x_ref, w_ref, out_ref):
    ls = [l0_ref[...], l1_ref[...], l2_ref[...]]
    mx = jnp.maximum(jnp.maximum(ls[0], ls[1]), ls[2])
    es = [jnp.exp(l - mx) for l in ls]
    den = es[0] + es[1] + es[2]
    os_ = [o0_ref[...], o1_ref[...], o2_ref[...]]
    mix = sum((e / den) * o.astype(F32) for e, o in zip(es, os_))
    out_ref[...] = x_ref[...] + _dot(mix.astype(BF16), w_ref[...])


def _dil_out(os_, ls, x, w):
    m = x.shape[0]
    tm = _row_tile(m)
    spec = pl.BlockSpec((tm, D_MODEL), lambda i: (i, 0))
    return pl.pallas_call(
        _dil_out_kernel,
        grid=(m // tm,),
        in_specs=[spec] * 7 + [_resident((D_MODEL, D_MODEL))],
        out_specs=spec,
        out_shape=jax.ShapeDtypeStruct((m, D_MODEL), F32),
        compiler_params=_cparams("parallel"),
        name="dil_out",
    )(*os_, *ls, x, w)


WIN_HEADS = 4


def _win_dec_kernel(qt_ref, knt_ref, vnt_ref, k0_ref, v0_ref, k1_ref, v1_ref, k2_ref, v2_ref,
                    o_ref, nk0_ref, nv0_ref, nk1_ref, nv1_ref, nk2_ref, nv2_ref):
    b, hc = pl.program_id(0), pl.program_id(1)
    rows = WIN_HEADS * HEAD_DIM
    cshape = (WIN_HEADS, HEAD_DIM, 1)
    caches = ((k0_ref, v0_ref, nk0_ref, nv0_ref), (k1_ref, v1_ref, nk1_ref, nv1_ref),
              (k2_ref, v2_ref, nk2_ref, nv2_ref))
    cols, ss, sn = [], [], []
    for g, (kc, _, _, _) in enumerate(caches):
        win, dil = DIL_PAIRS[g]
        sl = pl.ds(pl.multiple_of(g * D_MODEL + hc * rows, rows), rows)
        qc = _batch_column(qt_ref, b, sl).reshape(cshape)
        knc = _batch_column(knt_ref, b, sl).reshape(cshape)
        vnc = _batch_column(vnt_ref, b, sl).reshape(cshape)
        cols.append((knc, vnc))
        s = jnp.sum(kc[0].reshape(WIN_HEADS, HEAD_DIM, win) * qc, axis=1, keepdims=True)
        r = lax.broadcasted_iota(jnp.int32, s.shape, 2)
        ss.append(jnp.where((r & (dil - 1)) == 0, s, -jnp.inf))
        sn.append(jnp.sum(qc * knc, axis=1, keepdims=True))
    mx = sn[0]
    for g in range(N_DIL):
        mx = jnp.maximum(mx, jnp.maximum(sn[g], jnp.max(ss[g], axis=2, keepdims=True)))
    den = jnp.zeros((WIN_HEADS, 1, 1), F32)
    num = jnp.zeros(cshape, F32)
    for g, (kc, vc, nk, nv) in enumerate(caches):
        win, _ = DIL_PAIRS[g]
        knc, vnc = cols[g]
        p = jnp.exp(ss[g] - mx)
        pn = jnp.exp(sn[g] - mx)
        den = den + jnp.sum(p, axis=2, keepdims=True) + pn
        num = num + jnp.sum(p * vc[0].reshape(WIN_HEADS, HEAD_DIM, win), axis=2, keepdims=True) + pn * vnc
        last = lax.broadcasted_iota(jnp.int32, (rows, win), 1) == win - 1
        nk[0] = jnp.where(last, knc.reshape(rows, 1), pltpu.roll(kc[0], win - 1, 1))
        nv[0] = jnp.where(last, vnc.reshape(rows, 1), pltpu.roll(vc[0], win - 1, 1))
    o_ref[0] = jnp.broadcast_to(num / den, (WIN_HEADS, HEAD_DIM, LANE)).reshape(rows, LANE)


def _win_dec(qt, knt, vnt, caches):
    b = qt.shape[1]
    rows = WIN_HEADS * HEAD_DIM
    vec = pl.BlockSpec((N_DIL * D_MODEL, b), lambda i, hc: (0, 0))
    cspecs, cshapes = [], []
    for win, _ in DIL_PAIRS:
        cspecs += [pl.BlockSpec((1, rows, win), lambda i, hc: (i, hc, 0))] * 2
        cshapes += [jax.ShapeDtypeStruct((b, D_MODEL, win), F32)] * 2
    outs = pl.pallas_call(
        _win_dec_kernel,
        grid=(b, D_MODEL // rows),
        in_specs=[vec, vec, vec] + cspecs,
        out_specs=[pl.BlockSpec((1, rows, LANE), lambda i, hc: (i, hc, 0))] + cspecs,
        out_shape=[jax.ShapeDtypeStruct((b, D_MODEL, LANE), F32)] + cshapes,
        compiler_params=_cparams("parallel", "parallel"),
        name="win_dec",
    )(qt, knt, vnt, *caches)
    return outs[0], outs[1:]


def _prep_weights(W):
    P = {}
    row = lambda a: a.reshape(1, -1)
    for i in range(4):
        wup = W["f_w_up"][i].astype(BF16).reshape(D_MODEL, 2 * N_FF_CHUNK, FF_CHUNK).transpose(1, 0, 2)
        wcv = W["f_w_conv"][i].reshape(3, 2 * N_FF_CHUNK, FF_CHUNK).transpose(1, 0, 2)
        wcv = jnp.pad(wcv, ((0, 0), (0, SUBLANE - 3), (0, 0)))
        wdn = W["f_w_down"][i].astype(BF16).reshape(N_FF_CHUNK, FF_CHUNK, D_MODEL)
        P[f"ffn{i}"] = (row(W["norm_ffn"][i]), wup, wcv, wdn, row(W["norm_ple"][i]),
                        W["ple_w_gate"][i].astype(BF16), W["ple_w_proj"][i].astype(BF16), row(W["norm_final"]))
    P["norm_mix"] = [row(W["norm_mix"][i]) for i in range(4)]
    P["a_w_in"] = W["a_w_in"][0].astype(BF16)
    P["a_b_in"] = row(W["a_b_in"][0])
    P["a_w_dw"] = jnp.pad(W["a_w_dw"][0], ((0, CONV_HALO - CONV_A_WIDTH), (0, 0)))
    P["a_b_dw"] = row(W["a_b_dw"][0])
    P["a_ln_g"] = row(W["a_ln_g"][0])
    P["a_ln_b"] = row(W["a_ln_b"][0])
    P["a_w_out"] = W["a_w_out"][0].astype(BF16)
    P["b_w_qkv"] = W["b_w_qkv"][0].astype(BF16)
    P["b_w_o"] = W["b_w_o"][0].astype(BF16)
    P["s5"] = _s5_params(W["c_a_re"][0], W["c_a_im"][0], W["c_log_dt"][0], W["c_b_re"][0], W["c_b_im"][0],
                         W["c_c_re"][0], W["c_c_im"][0])
    P["c_d"] = row(W["c_d"][0])
    P["c_w_glu"] = W["c_w_glu"][0].astype(BF16)
    P["d_w_qkv"] = W["d_w_qkv"][0].astype(BF16)
    P["d_w_o"] = W["d_w_o"][0].astype(BF16)
    return P


def _heads(a, batch, rows):
    return a.reshape(1, batch, rows, N_HEADS, HEAD_DIM)


def _heads_t(a, batch, rows):
    return a.reshape(batch, N_HEADS, HEAD_DIM, rows).transpose(0, 3, 1, 2)[None]


def _cache_t(c):
    n, rows = c.shape[0], c.shape[1]
    return c.transpose(0, 2, 3, 1).reshape(n, D_MODEL, rows)


def _prompt_trunk(x, p, P):
    batch, seq, _ = x.shape
    m = batch * seq
    x = x.reshape(m, D_MODEL)
    tabs = _rope_tables(jnp.arange(seq, dtype=jnp.int32))
    st = {}
    u = _conv_in(x, P["norm_mix"][0], P["a_w_in"], P["a_b_in"])
    st["conv_a"] = u.reshape(batch, seq, D_MODEL)[None, :, seq - (CONV_A_WIDTH - 1):]
    x = _conv_out(u, x, seq, P["a_w_dw"], P["a_b_dw"], P["a_ln_g"], P["a_ln_b"], P["a_w_out"])
    ffn_states = []
    x, fs = _ffn_prompt(x, p[0].reshape(m, PLE_DIM), P["ffn0"], batch, seq, False)
    ffn_states.append(fs)
    q, k, v, kt, vt = _qkv(x, P["norm_mix"][1], P["b_w_qkv"], tabs, 1, seq, batch)
    st["moba_k"] = _heads_t(kt, batch, seq)
    st["moba_v"] = _heads_t(vt, batch, seq)
    o = _moba_attn(q.reshape(batch, seq, D_MODEL), k.reshape(batch, seq, D_MODEL),
                   v.reshape(batch, seq, D_MODEL), batch, seq)
    x = _proj(o.reshape(m, D_MODEL), x, P["b_w_o"])
    x, fs = _ffn_prompt(x, p[1].reshape(m, PLE_DIM), P["ffn1"], batch, seq, False)
    ffn_states.append(fs)
    s5 = P["s5"]
    bu_re, bu_im = _s5_in(x, P["norm_mix"][2], s5["wre"], s5["wim"])
    xre, xim = _s5_scan(bu_re, bu_im, s5["abar_re"], s5["abar_im"], batch, seq)
    st["s5_re"] = xre.reshape(batch, seq, S5_GROUPS, S5_STATE)[None, :, seq - 1]
    st["s5_im"] = xim.reshape(batch, seq, S5_GROUPS, S5_STATE)[None, :, seq - 1]
    x = _s5_out(x, P["norm_mix"][2], xre, xim, s5["cre"], s5["cim"], P["c_d"], P["c_w_glu"])
    x, fs = _ffn_prompt(x, p[2].reshape(m, PLE_DIM), P["ffn2"], batch, seq, False)
    ffn_states.append(fs)
    q, k, v, kt, vt = _qkv(x, P["norm_mix"][3], P["d_w_qkv"], tabs, N_DIL, seq, batch)
    os_, ls = [], []
    for g, (win, dil) in enumerate(DIL_PAIRS):
        keep = min(win, seq)
        feat = slice(g * D_MODEL, (g + 1) * D_MODEL)
        st[f"win_k{g + 1}"] = _heads_t(kt[:, feat, seq - keep:], batch, keep)
        st[f"win_v{g + 1}"] = _heads_t(vt[:, feat, seq - keep:], batch, keep)
        o, lse = _dil_attn(q.reshape(batch, seq, -1), k.reshape(batch, seq, -1), v.reshape(batch, seq, -1),
                           g, dil, batch, seq)
        os_.append(o)
        ls.append(lse)
    x = _dil_out(os_, ls, x, P["d_w_o"])
    x, fs = _ffn_prompt(x, p[3].reshape(m, PLE_DIM), P["ffn3"], batch, seq, True)
    ffn_states.append(fs)
    st["ffn_conv"] = jnp.stack(ffn_states, 0)
    return x.reshape(batch, seq, D_MODEL), st


def _sample_trunk(x, p, P, S, past_len):
    b = x.shape[0]
    x = x.reshape(b, D_MODEL)
    tabs = _rope_tables(jnp.full((b,), past_len, dtype=jnp.int32))
    st = {}
    buf = S["conv_a"][0]
    x, u = _conv_dec(x, P["norm_mix"][0], P["a_w_in"], P["a_b_in"], buf.transpose(1, 0, 2), P["a_w_dw"],
                     P["a_b_dw"], P["a_ln_g"], P["a_ln_b"], P["a_w_out"])
    st["conv_a"] = jnp.concatenate([buf[:, 1:], u[:, None]], axis=1)[None]
    ffn_states = []
    x, fs = _ffn_dec(x, p[0].reshape(b, PLE_DIM), P["ffn0"], S["ffn_conv"][0], False)
    ffn_states.append(fs)
    q, k, v = _qkv(x, P["norm_mix"][1], P["b_w_qkv"], tabs, 1, b)
    st["moba_k"] = _heads(k, b, 1)
    st["moba_v"] = _heads(v, b, 1)
    o = _moba_dec(q.astype(F32).T, k.T, v.T, _cache_t(S["moba_k"][0]), _cache_t(S["moba_v"][0]),
                  S["page_table"])
    x = _proj(o[:, :, 0].astype(BF16), x, P["b_w_o"])
    x, fs = _ffn_dec(x, p[1].reshape(b, PLE_DIM), P["ffn1"], S["ffn_conv"][1], False)
    ffn_states.append(fs)
    s5 = P["s5"]
    bu_re, bu_im = _s5_in(x, P["norm_mix"][2], s5["wre"], s5["wim"])
    xre, xim = _s5_step(bu_re, bu_im, S["s5_re"][0].reshape(b, S5_DIM), S["s5_im"][0].reshape(b, S5_DIM),
                        s5["abar_re"], s5["abar_im"])
    st["s5_re"] = xre.reshape(1, b, S5_GROUPS, S5_STATE)
    st["s5_im"] = xim.reshape(1, b, S5_GROUPS, S5_STATE)
    x = _s5_out(x, P["norm_mix"][2], xre, xim, s5["cre"], s5["cim"], P["c_d"], P["c_w_glu"])
    x, fs = _ffn_dec(x, p[2].reshape(b, PLE_DIM), P["ffn2"], S["ffn_conv"][2], False)
    ffn_states.append(fs)
    q, k, v = _qkv(x, P["norm_mix"][3], P["d_w_qkv"], tabs, N_DIL, b)
    caches = []
    for g in range(N_DIL):
        caches += [_cache_t(S[f"win_k{g + 1}"][0]), _cache_t(S[f"win_v{g + 1}"][0])]
    o, new = _win_dec(q.astype(F32).T, k.T, v.T, caches)
    for g, (win, _) in enumerate(DIL_PAIRS):
        st[f"win_k{g + 1}"] = _heads_t(new[2 * g], b, win)
        st[f"win_v{g + 1}"] = _heads_t(new[2 * g + 1], b, win)
    x = _proj(o[:, :, 0].astype(BF16), x, P["d_w_o"])
    x, fs = _ffn_dec(x, p[3].reshape(b, PLE_DIM), P["ffn3"], S["ffn_conv"][3], True)
    ffn_states.append(fs)
    st["ffn_conv"] = jnp.stack(ffn_states, 0)
    return x.reshape(b, 1, D_MODEL), st


_STATE_KEYS = ("conv_a", "moba_k", "moba_v", "s5_re", "s5_im", "win_k1", "win_v1", "win_k2", "win_v2",
               "win_k3", "win_v3", "ffn_conv")


def kernel(x_prompt, x_sample, state_conv_a, cache_moba_k, cache_moba_v, state_s5_re, state_s5_im, cache_win_k1, cache_win_v1, cache_win_k2, cache_win_v2, cache_win_k3, cache_win_v3, state_ffn_conv, page_table, p_prompt, p_sample, norm_mix, norm_ffn, norm_ple, norm_final, a_w_in, a_b_in, a_w_dw, a_b_dw, a_ln_g, a_ln_b, a_w_out, b_w_qkv, b_w_o, c_a_re, c_a_im, c_log_dt, c_b_re, c_b_im, c_c_re, c_c_im, c_d, c_w_glu, d_w_qkv, d_w_o, f_w_up, f_w_conv, f_w_down, ple_w_proj, ple_w_gate):
    W = dict(norm_mix=norm_mix, norm_ffn=norm_ffn, norm_ple=norm_ple, norm_final=norm_final,
             a_w_in=a_w_in, a_b_in=a_b_in, a_w_dw=a_w_dw, a_b_dw=a_b_dw, a_ln_g=a_ln_g, a_ln_b=a_ln_b,
             a_w_out=a_w_out, b_w_qkv=b_w_qkv, b_w_o=b_w_o,
             c_a_re=c_a_re, c_a_im=c_a_im, c_log_dt=c_log_dt, c_b_re=c_b_re, c_b_im=c_b_im,
             c_c_re=c_c_re, c_c_im=c_c_im, c_d=c_d, c_w_glu=c_w_glu,
             d_w_qkv=d_w_qkv, d_w_o=d_w_o, f_w_up=f_w_up, f_w_conv=f_w_conv, f_w_down=f_w_down,
             ple_w_proj=ple_w_proj, ple_w_gate=ple_w_gate)
    S = dict(conv_a=state_conv_a, moba_k=cache_moba_k, moba_v=cache_moba_v, s5_re=state_s5_re, s5_im=state_s5_im,
             win_k1=cache_win_k1, win_v1=cache_win_v1, win_k2=cache_win_k2, win_v2=cache_win_v2,
             win_k3=cache_win_k3, win_v3=cache_win_v3, ffn_conv=state_ffn_conv, page_table=page_table)
    P = _prep_weights(W)
    past_len = page_table.shape[1] * PAGE_SIZE
    y_prompt, sp = _prompt_trunk(x_prompt, p_prompt, P)
    y_sample, ss = _sample_trunk(x_sample, p_sample, P, S, past_len)
    return (y_prompt, y_sample) + tuple(sp[n] for n in _STATE_KEYS) + tuple(ss[n] for n in _STATE_KEYS)
```

```python
import functools
import math

import jax
import jax.numpy as jnp
from jax import lax
from jax.experimental import pallas as pl
from jax.experimental.pallas import tpu as pltpu

F32 = jnp.float32
BF16 = jnp.bfloat16

D_MODEL = 1024
N_HEADS = 16
HEAD_DIM = 64
ROT_DIM = 16
ROPE_THETA = 500000.0
EPS = 1e-6
PLE_DIM = 256
CONV_A_WIDTH = 31
MOBA_BLOCK = 256
MOBA_TOPK = 3
S5_GROUP = 16
S5_GROUPS = 64
S5_STATE = 64
S5_DIM = S5_GROUPS * S5_STATE
DIL_PAIRS = ((128, 1), (512, 4), (2048, 16))
N_DIL = 3
DIL_TAPS = 128
D_FF = 2816
PAGE_SIZE = 128

LANE = 128
SUBLANE = 8
FF_CHUNK = 256
N_FF_CHUNK = D_FF // FF_CHUNK
VMEM_LIMIT = 56 * 1024 * 1024
MASK_NEG = -30000.0

_NT = (((1,), (1,)), ((), ()))


def _cparams(*sem):
    return pltpu.CompilerParams(dimension_semantics=sem, vmem_limit_bytes=VMEM_LIMIT)


def _resident(shape):
    nd = len(shape)
    return pl.BlockSpec(shape, lambda *_: (0,) * nd, pipeline_mode=pl.Buffered(1))


def _rms(x, g):
    return x * lax.rsqrt(jnp.mean(x * x, axis=-1, keepdims=True) + EPS) * g


def _dot(a, b):
    return jnp.dot(a, b, preferred_element_type=F32)


def _row_tile(m):
    return 512 if m % 512 == 0 else m


def _conv_in_kernel(x_ref, g_ref, w_ref, b_ref, u_ref):
    h = _rms(x_ref[...], g_ref[...]).astype(BF16)
    z = _dot(h, w_ref[...]) + b_ref[...]
    u_ref[...] = z[:, :D_MODEL] * jax.nn.sigmoid(z[:, D_MODEL:])


def _conv_in(x, g, w, b):
    m = x.shape[0]
    tm = _row_tile(m)
    return pl.pallas_call(
        _conv_in_kernel,
        grid=(m // tm,),
        in_specs=[pl.BlockSpec((tm, D_MODEL), lambda i: (i, 0)),
                  _resident((1, D_MODEL)), _resident((D_MODEL, 2 * D_MODEL)), _resident((1, 2 * D_MODEL))],
        out_specs=pl.BlockSpec((tm, D_MODEL), lambda i: (i, 0)),
        out_shape=jax.ShapeDtypeStruct((m, D_MODEL), F32),
        compiler_params=_cparams("parallel"),
        name="conv_in",
    )(x, g, w, b)


def _ln_silu_out(cf, x, lng, lnb, wout):
    mu = jnp.mean(cf, axis=-1, keepdims=True)
    var = jnp.mean(jnp.square(cf - mu), axis=-1, keepdims=True)
    cn = (cf - mu) * lax.rsqrt(var + EPS) * lng + lnb
    return x + _dot(jax.nn.silu(cn).astype(BF16), wout)


CONV_HALO = 32


def _conv_out_kernel(u_ref, halo_ref, x_ref, wdw_ref, bdw_ref, lng_ref, lnb_ref, wout_ref, o_ref,
                     ext_ref, cf_ref, *, tm, tiles_per_seq):
    first = (pl.program_id(0) % tiles_per_seq) == 0
    ext_ref[0:CONV_HALO, :] = jnp.where(first, 0.0, halo_ref[...])
    ext_ref[CONV_HALO:, :] = u_ref[...]
    off = CONV_HALO - (CONV_A_WIDTH - 1)
    for c in range(D_MODEL // LANE):
        cols = slice(c * LANE, (c + 1) * LANE)
        acc = jnp.zeros((tm, LANE), F32)
        for j in range(CONV_A_WIDTH):
            acc = acc + wdw_ref[j:j + 1, cols] * ext_ref[pl.ds(off + j, tm), cols]
        cf_ref[:, cols] = acc + bdw_ref[:, cols]
    o_ref[...] = _ln_silu_out(cf_ref[...], x_ref[...], lng_ref[...], lnb_ref[...], wout_ref[...])


def _conv_out(u, x, seq, wdw, bdw, lng, lnb, wout):
    m = x.shape[0]
    tm = 256
    tps = seq // tm
    hb = tm // CONV_HALO
    kern = functools.partial(_conv_out_kernel, tm=tm, tiles_per_seq=tps)
    return pl.pallas_call(
        kern,
        grid=(m // tm,),
        in_specs=[pl.BlockSpec((tm, D_MODEL), lambda i: (i, 0)),
                  pl.BlockSpec((CONV_HALO, D_MODEL), lambda i: (jnp.maximum(i * hb - 1, 0), 0)),
                  pl.BlockSpec((tm, D_MODEL), lambda i: (i, 0)),
                  _resident((CONV_HALO, D_MODEL)), _resident((1, D_MODEL)), _resident((1, D_MODEL)),
                  _resident((1, D_MODEL)), _resident((D_MODEL, D_MODEL))],
        out_specs=pl.BlockSpec((tm, D_MODEL), lambda i: (i, 0)),
        out_shape=jax.ShapeDtypeStruct((m, D_MODEL), F32),
        scratch_shapes=[pltpu.VMEM((tm + CONV_HALO, D_MODEL), F32), pltpu.VMEM((tm, D_MODEL), F32)],
        compiler_params=_cparams("parallel"),
        name="conv_out",
    )(u, u, x, wdw, bdw, lng, lnb, wout)


def _conv_dec_kernel(x_ref, g_ref, win_ref, bin_ref, buf_ref, wdw_ref, bdw_ref, lng_ref, lnb_ref, wout_ref,
                     o_ref, u_ref, acc_ref):
    j = pl.program_id(0)

    @pl.when(j == 0)
    def _():
        h = _rms(x_ref[...], g_ref[...]).astype(BF16)
        z = _dot(h, win_ref[...]) + bin_ref[...]
        u = z[:, :D_MODEL] * jax.nn.sigmoid(z[:, D_MODEL:])
        u_ref[...] = u
        acc_ref[...] = wdw_ref[CONV_A_WIDTH - 1:CONV_A_WIDTH, :] * u

    acc_ref[...] += wdw_ref[pl.ds(j, 1), :] * buf_ref[0]

    @pl.when(j == CONV_A_WIDTH - 2)
    def _():
        cf = acc_ref[...] + bdw_ref[...]
        o_ref[...] = _ln_silu_out(cf, x_ref[...], lng_ref[...], lnb_ref[...], wout_ref[...])


def _conv_dec(x, g, win, bin_, buf_t, wdw, bdw, lng, lnb, wout):
    m = x.shape[0]
    return pl.pallas_call(
        _conv_dec_kernel,
        grid=(CONV_A_WIDTH - 1,),
        in_specs=[_resident((m, D_MODEL)), _resident((1, D_MODEL)), _resident((D_MODEL, 2 * D_MODEL)),
                  _resident((1, 2 * D_MODEL)),
                  pl.BlockSpec((1, m, D_MODEL), lambda j: (j, 0, 0)),
                  _resident((CONV_HALO, D_MODEL)), _resident((1, D_MODEL)), _resident((1, D_MODEL)),
                  _resident((1, D_MODEL)), _resident((D_MODEL, D_MODEL))],
        out_specs=[pl.BlockSpec((m, D_MODEL), lambda j: (0, 0)), pl.BlockSpec((m, D_MODEL), lambda j: (0, 0))],
        out_shape=[jax.ShapeDtypeStruct((m, D_MODEL), F32), jax.ShapeDtypeStruct((m, D_MODEL), F32)],
        scratch_shapes=[pltpu.VMEM((m, D_MODEL), F32)],
        compiler_params=_cparams("arbitrary"),
        name="conv_dec",
    )(x, g, win, bin_, buf_t, wdw, bdw, lng, lnb, wout)


def _ffn_tail(x2, p_ref, gple_ref, wgate_ref, wproj_ref, gfin_ref, final):
    hn = _rms(x2, gple_ref[...]).astype(BF16)
    gate = jax.nn.sigmoid(_dot(hn, wgate_ref[...]))
    x3 = x2 + _dot(p_ref[...].astype(BF16), wproj_ref[...]) * gate
    if final:
        x3 = _rms(x3, gfin_ref[...])
    return x3


def _ffn_prompt_kernel(x_ref, p_ref, gffn_ref, wup_ref, wcv_ref, wdn_ref, gple_ref, wgate_ref, wproj_ref,
                       gfin_ref, o_ref, st_ref, h_ref, act_ref, ua_ref, ub_ref, carry_ref,
                       *, tm, tiles_per_seq, final):
    t = pl.program_id(0) % tiles_per_seq

    @pl.when(t == 0)
    def _():
        carry_ref[...] = jnp.zeros_like(carry_ref)

    h_ref[...] = _rms(x_ref[...], gffn_ref[...]).astype(BF16)

    def conv_half(c, ubuf):
        u = _dot(h_ref[...], wup_ref[:, _ff_cols(c)])
        ubuf[0:SUBLANE, :] = carry_ref[c]
        ubuf[SUBLANE:, :] = u
        w = wcv_ref[:, _ff_cols(c)]
        cc = (w[0:1] * ubuf[pl.ds(SUBLANE - 2, tm), :] + w[1:2] * ubuf[pl.ds(SUBLANE - 1, tm), :]
              + w[2:3] * u)
        carry_ref[c] = ubuf[pl.ds(tm, SUBLANE), :]
        return cc

    def chunk(j, slot):
        ca = conv_half(j, ua_ref.at[slot])
        cb = conv_half(j + N_FF_CHUNK, ub_ref.at[slot])
        act_ref[j] = (jax.nn.silu(ca) * cb).astype(BF16)

    for j in range(N_FF_CHUNK):
        chunk(j, j % 2)

    @pl.when(t == tiles_per_seq - 1)
    def _():
        st_ref[0] = carry_ref[...]

    y = _dot(act_ref[0], wdn_ref[_ff_rows(0), :])
    for j in range(1, N_FF_CHUNK):
        y = y + _dot(act_ref[j], wdn_ref[_ff_rows(j), :])
    o_ref[...] = _ffn_tail(x_ref[...] + y, p_ref, gple_ref, wgate_ref, wproj_ref, gfin_ref, final)


def _ffn_dec_kernel(x_ref, p_ref, gffn_ref, wup_ref, wcv_ref, wdn_ref, gple_ref, wgate_ref, wproj_ref,
                    gfin_ref, s0_ref, s1_ref, o_ref, u_ref, h_ref, *, final):
    h_ref[...] = _rms(x_ref[...], gffn_ref[...]).astype(BF16)

    def conv_half(c):
        u = _dot(h_ref[...], wup_ref[:, _ff_cols(c)])
        u_ref[c] = u
        w = wcv_ref[:, _ff_cols(c)]
        return w[0:1] * s0_ref[c] + w[1:2] * s1_ref[c] + w[2:3] * u

    y = None
    for j in range(N_FF_CHUNK):
        act = (jax.nn.silu(conv_half(j)) * conv_half(j + N_FF_CHUNK)).astype(BF16)
        d = _dot(act, wdn_ref[_ff_rows(j), :])
        y = d if y is None else y + d
    o_ref[...] = _ffn_tail(x_ref[...] + y, p_ref, gple_ref, wgate_ref, wproj_ref, gfin_ref, final)


def _ff_cols(c):
    return slice(c * FF_CHUNK, (c + 1) * FF_CHUNK)


_ff_rows = _ff_cols


def _ffn_weight_specs():
    return [_resident((1, D_MODEL)),
            _resident((D_MODEL, 2 * D_FF)),
            _resident((SUBLANE, 2 * D_FF)),
            _resident((D_FF, D_MODEL)),
            _resident((1, D_MODEL)), _resident((D_MODEL, D_MODEL)), _resident((PLE_DIM, D_MODEL)),
            _resident((1, D_MODEL))]


def _ffn_prompt(x, p, wts, batch, seq, final):
    m = x.shape[0]
    tm = _row_tile(seq)
    tps = seq // tm
    kern = functools.partial(_ffn_prompt_kernel, tm=tm, tiles_per_seq=tps, final=final)
    nc = 2 * N_FF_CHUNK
    out, st = pl.pallas_call(
        kern,
        grid=(m // tm,),
        in_specs=[pl.BlockSpec((tm, D_MODEL), lambda i: (i, 0)),
                  pl.BlockSpec((tm, PLE_DIM), lambda i: (i, 0))] + _ffn_weight_specs(),
        out_specs=[pl.BlockSpec((tm, D_MODEL), lambda i: (i, 0)),
                   pl.BlockSpec((1, nc, SUBLANE, FF_CHUNK), lambda i: (i // tps, 0, 0, 0))],
        out_shape=[jax.ShapeDtypeStruct((m, D_MODEL), F32),
                   jax.ShapeDtypeStruct((batch, nc, SUBLANE, FF_CHUNK), F32)],
        scratch_shapes=[pltpu.VMEM((tm, D_MODEL), BF16), pltpu.VMEM((N_FF_CHUNK, tm, FF_CHUNK), BF16),
                        pltpu.VMEM((2, tm + SUBLANE, FF_CHUNK), F32), pltpu.VMEM((2, tm + SUBLANE, FF_CHUNK), F32),
                        pltpu.VMEM((nc, SUBLANE, FF_CHUNK), F32)],
        compiler_params=_cparams("arbitrary"),
        name="ffn_prompt",
    )(x, p, *wts)
    new_state = st[:, :, SUBLANE - 2:, :].transpose(0, 2, 1, 3).reshape(batch, 2, 2 * D_FF)
    return out, new_state


def _ffn_dec(x, p, wts, state, final):
    m = x.shape[0]
    nc = 2 * N_FF_CHUNK
    kern = functools.partial(_ffn_dec_kernel, final=final)
    s = state.reshape(m, 2, nc, FF_CHUNK).transpose(1, 2, 0, 3)
    out, u = pl.pallas_call(
        kern,
        grid=(1,),
        in_specs=[_resident((m, D_MODEL)), _resident((m, PLE_DIM))] + _ffn_weight_specs()
        + [_resident((nc, m, FF_CHUNK)), _resident((nc, m, FF_CHUNK))],
        out_specs=[pl.BlockSpec((m, D_MODEL), lambda i: (0, 0)),
                   pl.BlockSpec((nc, m, FF_CHUNK), lambda i: (0, 0, 0))],
        out_shape=[jax.ShapeDtypeStruct((m, D_MODEL), F32), jax.ShapeDtypeStruct((nc, m, FF_CHUNK), F32)],
        scratch_shapes=[pltpu.VMEM((m, D_MODEL), BF16)],
        compiler_params=_cparams("arbitrary"),
        name="ffn_dec",
    )(x, p, *wts, s[0], s[1])
    u_rows = u.transpose(1, 0, 2).reshape(m, 1, 2 * D_FF)
    return out, jnp.concatenate([state[:, 1:], u_rows], axis=1)


def _qkv_kernel(x_ref, g_ref, w_ref, c_ref, s1_ref, s2_ref, q_ref, k_ref, v_ref, *t_refs):
    h = _rms(x_ref[...], g_ref[...]).astype(BF16)
    reps = D_MODEL // LANE
    cos = jnp.concatenate([c_ref[...]] * reps, axis=1)
    sn1 = jnp.concatenate([s1_ref[...]] * reps, axis=1)
    sn2 = jnp.concatenate([s2_ref[...]] * reps, axis=1)
    half = ROT_DIM // 2

    def rope(z):
        return (z * cos + pltpu.roll(z, D_MODEL - half, 1) * sn1 + pltpu.roll(z, half, 1) * sn2)

    q = rope(_dot(h, w_ref[:, 0:D_MODEL]))
    q_ref[...] = (q * (HEAD_DIM ** -0.5)).astype(BF16)
    k = rope(_dot(h, w_ref[:, D_MODEL:2 * D_MODEL]))
    v = _dot(h, w_ref[:, 2 * D_MODEL:])
    k_ref[...] = k.astype(k_ref.dtype)
    v_ref[...] = v.astype(v_ref.dtype)
    if t_refs:
        kt_ref, vt_ref = t_refs
        kt_ref[0] = k.T
        vt_ref[0] = v.T


def _qkv(x, g, w, tabs, n_groups, seq, batch=None):
    m = x.shape[0]
    tm = _row_tile(seq)
    npb = seq // tm
    row = lambda gi, i: (i, 0)
    col = lambda gi, i: (i, gi)
    tab = pl.BlockSpec((tm, LANE), lambda gi, i: (i % npb, 0))
    n = n_groups * D_MODEL
    if batch is None:
        out_specs = [pl.BlockSpec((tm, D_MODEL), col)] * 3
        out_shape = [jax.ShapeDtypeStruct((m, n), BF16)] + [jax.ShapeDtypeStruct((m, n), F32)] * 2
    else:
        out_specs = [pl.BlockSpec((None, tm, D_MODEL), lambda gi, i: (gi, i, 0))] * 3
        out_shape = [jax.ShapeDtypeStruct((n_groups, m, D_MODEL), BF16)] * 3
        out_specs += [pl.BlockSpec((1, D_MODEL, tm), lambda gi, i: (i // npb, gi, i % npb))] * 2
        out_shape += [jax.ShapeDtypeStruct((batch, n, seq), F32)] * 2
    return pl.pallas_call(
        _qkv_kernel,
        grid=(n_groups, m // tm),
        in_specs=[pl.BlockSpec((tm, D_MODEL), row), pl.BlockSpec((1, D_MODEL), lambda gi, i: (0, 0)),
                  pl.BlockSpec((D_MODEL, 3 * D_MODEL), lambda gi, i: (0, gi)), tab, tab, tab],
        out_specs=out_specs,
        out_shape=out_shape,
        compiler_params=_cparams("parallel", "parallel"),
        name="qkv_rope",
    )(x, g, w, *tabs)


def _rope_tables(pos):
    half = ROT_DIM // 2
    inv_freq = ROPE_THETA ** (-jnp.arange(half, dtype=F32) / half)
    ang = pos.astype(F32)[:, None] * inv_freq[None, :]
    cos, sin = jnp.cos(ang), jnp.sin(ang)
    lane = jnp.arange(LANE) % HEAD_DIM
    idx = lane % half
    c = jnp.where(lane[None, :] < ROT_DIM, cos[:, idx], 1.0)
    s1 = jnp.where(lane[None, :] < half, -sin[:, idx], 0.0)
    s2 = jnp.where((lane[None, :] >= half) & (lane[None, :] < ROT_DIM), sin[:, idx], 0.0)
    return c, s1, s2


def _proj_kernel(a_ref, x_ref, w_ref, o_ref):
    o_ref[...] = x_ref[...] + _dot(a_ref[...], w_ref[...])


def _proj(a, x, w):
    m = x.shape[0]
    tm = _row_tile(m)
    return pl.pallas_call(
        _proj_kernel,
        grid=(m // tm,),
        in_specs=[pl.BlockSpec((tm, D_MODEL), lambda i: (i, 0)), pl.BlockSpec((tm, D_MODEL), lambda i: (i, 0)),
                  _resident((D_MODEL, D_MODEL))],
        out_specs=pl.BlockSpec((tm, D_MODEL), lambda i: (i, 0)),
        out_shape=jax.ShapeDtypeStruct((m, D_MODEL), F32),
        compiler_params=_cparams("parallel"),
        name="out_proj",
    )(a, x, w)


def _moba_kernel(q_ref, k_ref, v_ref, o_ref, ka0_ref, ka1_ref, km_ref, *, seq):
    nb = seq // MOBA_BLOCK
    qi = pl.program_id(2)
    blk = MOBA_BLOCK
    span = 2 * blk

    @pl.when(qi == 0)
    def _():
        kf = k_ref[0].astype(F32)
        rblk = lax.broadcasted_iota(jnp.int32, (seq, LANE), 0) >> int(math.log2(blk))
        lane = lax.broadcasted_iota(jnp.int32, (seq, LANE), 1)
        ka0_ref[...] = jnp.where(lane < HEAD_DIM, kf, (lane - HEAD_DIM == rblk).astype(F32)).astype(BF16)
        ka1_ref[...] = jnp.where(lane >= HEAD_DIM, kf, (lane == rblk).astype(F32)).astype(BF16)
        km_ref[...] = jnp.mean(kf.reshape(nb, blk, LANE), axis=1)

    q2 = q_ref[0]
    lane_q = lax.broadcasted_iota(jnp.int32, (blk, LANE), 1)
    lane_k = lax.broadcasted_iota(jnp.int32, (nb, LANE), 1)
    n_iota = lax.broadcasted_iota(jnp.int32, (nb, blk), 0)
    row = lax.broadcasted_iota(jnp.int32, (blk, blk), 0)
    colk = lax.broadcasted_iota(jnp.int32, (blk, blk), 1)
    km = km_ref[...]
    start = pl.multiple_of(qi * blk, blk)
    v_own = v_ref[0, pl.ds(start, blk), :]
    qas, init = [], []
    for hd, ka_ref in ((0, ka0_ref), (1, ka1_ref)):
        mine_q = (lane_q < HEAD_DIM) if hd == 0 else (lane_q >= HEAD_DIM)
        mine_k = (lane_k < HEAD_DIM) if hd == 0 else (lane_k >= HEAD_DIM)
        kmh = jnp.where(mine_k, km, 0.0)
        hi = kmh.astype(BF16)
        lo = (kmh - hi.astype(F32)).astype(BF16)
        st = (lax.dot_general(hi, q2, _NT, preferred_element_type=F32)
              + lax.dot_general(lo, q2, _NT, preferred_element_type=F32))
        s = jnp.where(n_iota < qi, st, -jnp.inf)
        pen = jnp.full((nb, blk), MASK_NEG, F32)
        for _ in range(MOBA_TOPK):
            mx = jnp.max(s, axis=0, keepdims=True)
            idx = jnp.min(jnp.where(s == mx, n_iota, nb), axis=0, keepdims=True)
            chosen = jnp.logical_and(n_iota == idx, mx > -jnp.inf)
            pen = jnp.where(chosen, 0.0, pen)
            s = jnp.where(chosen, -jnp.inf, s)
        n_lo = HEAD_DIM if hd == 0 else 0
        parts = [jnp.zeros((n_lo, blk), F32)] if n_lo else []
        parts += [pen, jnp.zeros((LANE - nb - n_lo, blk), F32)]
        pen_q = jnp.concatenate(parts, axis=0).T
        qas.append(jnp.where(mine_q, q2, pen_q.astype(BF16)))

        qd = jnp.where(mine_q, q2, jnp.zeros_like(q2))
        s0 = lax.dot_general(qd, ka_ref[pl.ds(start, blk), :], _NT, preferred_element_type=F32)
        s0 = jnp.where(colk <= row, s0, -jnp.inf)
        m0 = jnp.max(s0, axis=1, keepdims=True)
        p0 = jnp.exp(s0 - m0)
        init += [m0, jnp.sum(p0, axis=1, keepdims=True), _dot(p0.astype(BF16), v_own)]

    def past(j, carry):
        off = pl.multiple_of(j * span, span)
        vj = v_ref[0, pl.ds(off, span), :]
        new = []
        for hd, ka_ref in ((0, ka0_ref), (1, ka1_ref)):
            m, l, acc = carry[3 * hd:3 * hd + 3]
            sn = lax.dot_general(qas[hd], ka_ref[pl.ds(off, span), :], _NT, preferred_element_type=F32)
            mn = jnp.maximum(m, jnp.max(sn, axis=1, keepdims=True))
            alpha = jnp.exp(m - mn)
            p = jnp.exp(sn - mn)
            new += [mn, alpha * l + jnp.sum(p, axis=1, keepdims=True), alpha * acc + _dot(p.astype(BF16), vj)]
        return tuple(new)

    fin = lax.fori_loop(0, (qi + 1) // 2, past, tuple(init))
    o_ref[0] = jnp.where(lane_q < HEAD_DIM, fin[2] / fin[1], fin[5] / fin[4]).astype(BF16)


def _moba_attn(q, k, v, batch, seq):
    nhp = D_MODEL // LANE
    nq = seq // MOBA_BLOCK
    kern = functools.partial(_moba_kernel, seq=seq)
    qspec = pl.BlockSpec((1, MOBA_BLOCK, LANE), lambda b, hp, qi: (b, qi, hp))
    kvspec = pl.BlockSpec((1, seq, LANE), lambda b, hp, qi: (b, 0, hp))
    return pl.pallas_call(
        kern,
        grid=(batch, nhp, nq),
        in_specs=[qspec, kvspec, kvspec],
        out_specs=qspec,
        out_shape=jax.ShapeDtypeStruct((batch, seq, D_MODEL), BF16),
        scratch_shapes=[pltpu.VMEM((seq, LANE), BF16), pltpu.VMEM((seq, LANE), BF16),
                        pltpu.VMEM((seq // MOBA_BLOCK, LANE), F32)],
        compiler_params=_cparams("parallel", "parallel", "arbitrary"),
        name="moba_attn",
    )(q, k, v)


def _batch_column(ref, b, rows=None):
    tile = ref[...] if rows is None else ref[rows, :]
    lane = lax.broadcasted_iota(jnp.int32, tile.shape, 1)
    return jnp.sum(jnp.where(lane == b, tile, 0.0), axis=1, keepdims=True)


MOBA_DEC_PAGES = 4


def _moba_dec_kernel(pt_ref, qt_ref, knt_ref, vnt_ref, *refs, n_blk):
    k_refs, v_refs = refs[:MOBA_DEC_PAGES], refs[MOBA_DEC_PAGES:2 * MOBA_DEC_PAGES]
    o_ref, qb_ref, s_ref, vall_ref = refs[2 * MOBA_DEC_PAGES:]
    b, n = pl.program_id(0), pl.program_id(1)
    hshape = (N_HEADS, HEAD_DIM, PAGE_SIZE)

    @pl.when(n == 0)
    def _():
        qb_ref[...] = jnp.broadcast_to(_batch_column(qt_ref, b), (D_MODEL, PAGE_SIZE))

    qb = qb_ref[...].reshape(hshape)
    for j, (kr, vr) in enumerate(zip(k_refs, v_refs)):
        pg = MOBA_DEC_PAGES * n + j
        s_ref[pg] = jnp.sum(kr[0].reshape(hshape) * qb, axis=1, keepdims=True)
        vall_ref[pg] = vr[0]

    @pl.when(n == pl.num_programs(1) - 1)
    def _():
        cshape = (N_HEADS, HEAD_DIM, 1)
        qc = qb_ref[:, 0:1].reshape(cshape)
        knc = _batch_column(knt_ref, b).reshape(cshape)
        vnc = _batch_column(vnt_ref, b).reshape(cshape)
        s_own = jnp.sum(qc * knc, axis=1, keepdims=True)
        bsum, bmax = [], []
        for i in range(n_blk):
            a, c = s_ref[2 * i], s_ref[2 * i + 1]
            bsum.append(jnp.sum(a + c, axis=2, keepdims=True))
            bmax.append(jnp.max(jnp.maximum(a, c), axis=2, keepdims=True))
        sels = []
        for i in range(n_blk):
            rank = jnp.zeros((N_HEADS, 1, 1), F32)
            for m in range(n_blk):
                if m != i:
                    beats = (bsum[m] >= bsum[i]) if m < i else (bsum[m] > bsum[i])
                    rank = rank + jnp.where(beats, 1.0, 0.0)
            sels.append(rank < MOBA_TOPK)
        mx = s_own
        for i in range(n_blk):
            mx = jnp.maximum(mx, jnp.where(sels[i], bmax[i], -jnp.inf))
        w_own = jnp.exp(s_own - mx)
        lacc = jnp.zeros((N_HEADS, 1, PAGE_SIZE), F32)
        vacc = jnp.zeros(hshape, F32)
        for pg in range(2 * n_blk):
            p = jnp.where(sels[pg // 2], jnp.exp(s_ref[pg] - mx), 0.0)
            lacc = lacc + p
            vacc = vacc + p * vall_ref[pg].reshape(hshape)
        den = w_own + jnp.sum(lacc, axis=2, keepdims=True)
        o = (w_own * vnc + jnp.sum(vacc, axis=2, keepdims=True)) / den
        o_ref[0] = jnp.broadcast_to(o, hshape).reshape(D_MODEL, PAGE_SIZE)


def _moba_dec(qt, knt, vnt, cache_k, cache_v, page_table):
    b, n_pages = page_table.shape
    ppb = MOBA_BLOCK // PAGE_SIZE
    n_blk = n_pages // ppb
    pt = page_table.reshape(-1)
    vec = pl.BlockSpec((D_MODEL, b), lambda i, n, pt: (0, 0))

    def page(off):
        return pl.BlockSpec((1, D_MODEL, PAGE_SIZE),
                            lambda i, n, pt: (pt[i * n_pages + n * MOBA_DEC_PAGES + off], 0, 0))

    pages = [page(off) for off in range(MOBA_DEC_PAGES)]
    kern = functools.partial(_moba_dec_kernel, n_blk=n_blk)
    return pl.pallas_call(
        kern,
        grid_spec=pltpu.PrefetchScalarGridSpec(
            num_scalar_prefetch=1,
            grid=(b, n_pages // MOBA_DEC_PAGES),
            in_specs=[vec, vec, vec] + pages + pages,
            out_specs=pl.BlockSpec((1, D_MODEL, PAGE_SIZE), lambda i, n, pt: (i, 0, 0)),
            scratch_shapes=[pltpu.VMEM((D_MODEL, PAGE_SIZE), F32),
                            pltpu.VMEM((n_pages, N_HEADS, 1, PAGE_SIZE), F32),
                            pltpu.VMEM((n_pages, D_MODEL, PAGE_SIZE), F32)]),
        out_shape=jax.ShapeDtypeStruct((b, D_MODEL, PAGE_SIZE), F32),
        compiler_params=_cparams("parallel", "arbitrary"),
        name="moba_dec",
    )(pt, qt, knt, vnt, *([cache_k] * MOBA_DEC_PAGES), *([cache_v] * MOBA_DEC_PAGES))


S5_SLAB = 8
S5_SLAB_W = S5_DIM // S5_SLAB


def _s5_in_kernel(x_ref, g_ref, wre_ref, wim_ref, re_ref, im_ref):
    h = _rms(x_ref[...], g_ref[...]).astype(BF16)
    for j in range(S5_SLAB):
        hj = h[:, j * LANE:(j + 1) * LANE]
        cols = slice(j * S5_SLAB_W, (j + 1) * S5_SLAB_W)
        re_ref[:, cols] = _dot(hj, wre_ref[j])
        im_ref[:, cols] = _dot(hj, wim_ref[j])


def _s5_in(x, g, wre, wim):
    m = x.shape[0]
    tm = _row_tile(m)
    spec = pl.BlockSpec((tm, S5_DIM), lambda i: (i, 0))
    return pl.pallas_call(
        _s5_in_kernel,
        grid=(m // tm,),
        in_specs=[pl.BlockSpec((tm, D_MODEL), lambda i: (i, 0)), _resident((1, D_MODEL)),
                  _resident((S5_SLAB, LANE, S5_SLAB_W)), _resident((S5_SLAB, LANE, S5_SLAB_W))],
        out_specs=[spec, spec],
        out_shape=[jax.ShapeDtypeStruct((m, S5_DIM), F32)] * 2,
        compiler_params=_cparams("parallel"),
        name="s5_in",
    )(x, g, wre, wim)


def _s5_scan_kernel(bre_ref, bim_ref, ar_ref, ai_ref, xre_ref, xim_ref, st_ref, *, batch, steps):
    @pl.when(pl.program_id(0) == 0)
    def _():
        st_ref[...] = jnp.zeros_like(st_ref)

    ar = ar_ref[...]
    ai = ai_ref[...]

    def step(t, carry):
        new = []
        for b in range(batch):
            xr, xi = carry[2 * b], carry[2 * b + 1]
            nr = ar * xr - ai * xi + bre_ref[b, t]
            ni = ar * xi + ai * xr + bim_ref[b, t]
            xre_ref[b, t] = nr
            xim_ref[b, t] = ni
            new += [nr, ni]
        return tuple(new)

    init = tuple(st_ref[i] for i in range(2 * batch))
    fin = lax.fori_loop(0, steps, step, init)
    for i in range(2 * batch):
        st_ref[i] = fin[i]


def _s5_scan(bu_re, bu_im, abar_re, abar_im, batch, seq):
    steps = 32
    shape = (batch, seq, S5_SLAB, S5_SLAB_W)
    spec = pl.BlockSpec((batch, steps, S5_SLAB, S5_SLAB_W), lambda t: (0, t, 0, 0))
    kern = functools.partial(_s5_scan_kernel, batch=batch, steps=steps)
    xre, xim = pl.pallas_call(
        kern,
        grid=(seq // steps,),
        in_specs=[spec, spec, _resident((S5_SLAB, S5_SLAB_W)), _resident((S5_SLAB, S5_SLAB_W))],
        out_specs=[spec, spec],
        out_shape=[jax.ShapeDtypeStruct(shape, F32)] * 2,
        scratch_shapes=[pltpu.VMEM((2 * batch, S5_SLAB, S5_SLAB_W), F32)],
        compiler_params=_cparams("arbitrary"),
        name="s5_scan",
    )(bu_re.reshape(shape), bu_im.reshape(shape), abar_re, abar_im)
    return xre.reshape(batch * seq, S5_DIM), xim.reshape(batch * seq, S5_DIM)


def _s5_step_kernel(bre_ref, bim_ref, hr_ref, hi_ref, ar_ref, ai_ref, xre_ref, xim_ref):
    ar, ai, hr, hi = ar_ref[...], ai_ref[...], hr_ref[...], hi_ref[...]
    xre_ref[...] = ar * hr - ai * hi + bre_ref[...]
    xim_ref[...] = ar * hi + ai * hr + bim_ref[...]


def _s5_step(bu_re, bu_im, h_re, h_im, abar_re, abar_im):
    m = bu_re.shape[0]
    full = _resident((m, S5_DIM))
    vec = _resident((1, S5_DIM))
    return pl.pallas_call(
        _s5_step_kernel,
        grid=(1,),
        in_specs=[full, full, full, full, vec, vec],
        out_specs=[pl.BlockSpec((m, S5_DIM), lambda i: (0, 0))] * 2,
        out_shape=[jax.ShapeDtypeStruct((m, S5_DIM), F32)] * 2,
        compiler_params=_cparams("arbitrary"),
        name="s5_step",
    )(bu_re, bu_im, h_re, h_im, abar_re.reshape(1, S5_DIM), abar_im.reshape(1, S5_DIM))


def _s5_out_kernel(x_ref, g_ref, xre_ref, xim_ref, cre_ref, cim_ref, d_ref, wglu_ref, o_ref):
    x = x_ref[...]
    h = _rms(x, g_ref[...])
    ys = []
    for j in range(S5_SLAB):
        cols = slice(j * S5_SLAB_W, (j + 1) * S5_SLAB_W)
        ys.append(_dot(xre_ref[:, cols].astype(BF16), cre_ref[j]) - _dot(xim_ref[:, cols].astype(BF16), cim_ref[j]))
    y = jnp.concatenate(ys, axis=1) + d_ref[...] * h
    gl = _dot(jax.nn.gelu(y).astype(BF16), wglu_ref[...])
    o_ref[...] = x + gl[:, :D_MODEL] * jax.nn.sigmoid(gl[:, D_MODEL:])


def _s5_out(x, g, xre, xim, cre, cim, d, wglu):
    m = x.shape[0]
    tm = _row_tile(m)
    sspec = pl.BlockSpec((tm, S5_DIM), lambda i: (i, 0))
    return pl.pallas_call(
        _s5_out_kernel,
        grid=(m // tm,),
        in_specs=[pl.BlockSpec((tm, D_MODEL), lambda i: (i, 0)), _resident((1, D_MODEL)), sspec, sspec,
                  _resident((S5_SLAB, S5_SLAB_W, LANE)), _resident((S5_SLAB, S5_SLAB_W, LANE)),
                  _resident((1, D_MODEL)), _resident((D_MODEL, 2 * D_MODEL))],
        out_specs=pl.BlockSpec((tm, D_MODEL), lambda i: (i, 0)),
        out_shape=jax.ShapeDtypeStruct((m, D_MODEL), F32),
        compiler_params=_cparams("parallel"),
        name="s5_out",
    )(x, g, xre, xim, cre, cim, d, wglu)


def _s5_params(a_re, a_im, log_dt, b_re, b_im, c_re, c_im):
    dt = jnp.exp(log_dt)[:, None]
    mag = jnp.exp(a_re * dt)
    abar_re = mag * jnp.cos(a_im * dt)
    abar_im = mag * jnp.sin(a_im * dt)
    den = a_re * a_re + a_im * a_im
    f_re = ((abar_re - 1.0) * a_re + abar_im * a_im) / den
    f_im = (abar_im * a_re - (abar_re - 1.0) * a_im) / den
    bbar_re = f_re[..., None] * b_re - f_im[..., None] * b_im
    bbar_im = f_re[..., None] * b_im + f_im[..., None] * b_re
    eye = jnp.eye(S5_SLAB, dtype=F32)

    def in_w(bb):
        bb = bb.reshape(S5_SLAB, S5_SLAB, S5_STATE, S5_GROUP)
        w = jnp.einsum("jgpc,gh->jgchp", bb, eye)
        return w.reshape(S5_SLAB, LANE, S5_SLAB_W).astype(BF16)

    def out_w(cc):
        cc = cc.reshape(S5_SLAB, S5_SLAB, S5_GROUP, S5_STATE)
        w = jnp.einsum("jgcp,gh->jgphc", cc, eye)
        return w.reshape(S5_SLAB, S5_SLAB_W, LANE).astype(BF16)

    return dict(abar_re=abar_re.reshape(S5_SLAB, S5_SLAB_W), abar_im=abar_im.reshape(S5_SLAB, S5_SLAB_W),
                wre=in_w(bbar_re), wim=in_w(bbar_im), cre=out_w(c_re), cim=out_w(c_im))


DIL_TQ = 128


def _dil_kernel(q_ref, kp_ref, kc_ref, vp_ref, vc_ref, o_ref, lse_ref, *, tq):
    t = pl.program_id(3)
    q_all = q_ref[0]
    kcat = jnp.concatenate([kp_ref[0], kc_ref[0]], axis=0).astype(BF16)
    vcat = jnp.concatenate([vp_ref[0], vc_ref[0]], axis=0).astype(BF16)
    lane = lax.broadcasted_iota(jnp.int32, (DIL_TQ, LANE), 1)
    row = lax.broadcasted_iota(jnp.int32, (DIL_TQ, 2 * DIL_TQ), 0)
    col = lax.broadcasted_iota(jnp.int32, (DIL_TQ, 2 * DIL_TQ), 1)
    band = jnp.logical_and(col >= row, col <= row + DIL_TAPS)
    for sub in range(tq // DIL_TQ):
        q = q_all[sub * DIL_TQ:(sub + 1) * DIL_TQ]
        keys = kcat[sub * DIL_TQ:(sub + 2) * DIL_TQ]
        vals = vcat[sub * DIL_TQ:(sub + 2) * DIL_TQ]
        ok = band
        if sub == 0:
            ok = jnp.logical_and(band, jnp.logical_or(t > 0, col >= DIL_TQ))
        o_h, lse_h = [], []
        for hd in range(2):
            mine = (lane < HEAD_DIM) if hd == 0 else (lane >= HEAD_DIM)
            qh = jnp.where(mine, q, jnp.zeros_like(q))
            s = lax.dot_general(qh, keys, _NT, preferred_element_type=F32)
            s = jnp.where(ok, s, -jnp.inf)
            m = jnp.max(s, axis=1, keepdims=True)
            p = jnp.exp(s - m)
            l = jnp.sum(p, axis=1, keepdims=True)
            o_h.append(_dot(p.astype(BF16), vals) / l)
            lse_h.append(m + jnp.log(l))
        rows = slice(sub * DIL_TQ, (sub + 1) * DIL_TQ)
        o_ref[0, rows, :] = jnp.where(lane < HEAD_DIM, o_h[0], o_h[1]).astype(BF16)
        lse_ref[0, rows, :] = jnp.where(lane < HEAD_DIM, lse_h[0], lse_h[1])


def _dil_attn(q, k, v, g, dil, batch, seq):
    sd = seq // dil
    tq = min(512, sd)
    nhp = D_MODEL // LANE
    qv = q.reshape(batch, sd, dil * D_MODEL)
    kv = k.reshape(batch, sd, dil * D_MODEL)
    vv = v.reshape(batch, sd, dil * D_MODEL)
    sub = tq // DIL_TQ
    cur = pl.BlockSpec((1, tq, LANE), lambda b, r, hp, t: (b, t, r * nhp + hp))
    prev = pl.BlockSpec((1, DIL_TQ, LANE), lambda b, r, hp, t: (b, jnp.maximum(t * sub - 1, 0), r * nhp + hp))
    ospec = cur
    kern = functools.partial(_dil_kernel, tq=tq)
    o, lse = pl.pallas_call(
        kern,
        grid=(batch, dil, nhp, sd // tq),
        in_specs=[cur, prev, cur, prev, cur],
        out_specs=[ospec, ospec],
        out_shape=[jax.ShapeDtypeStruct((batch, sd, dil * D_MODEL), BF16),
                   jax.ShapeDtypeStruct((batch, sd, dil * D_MODEL), F32)],
        compiler_params=_cparams("parallel", "parallel", "parallel", "parallel"),
        name=f"dil_attn{g}",
    )(qv, kv, kv, vv, vv)
    return o.reshape(batch * seq, D_MODEL), lse.reshape(batch * seq, D_MODEL)


def _dil_out_kernel(o0_ref, o1_ref, o2_ref, l0_ref, l1_ref, l2_ref, x_ref, w_ref, out_ref):
    ls = [l0_ref[...], l1_ref[...], l2_ref[...]]
    mx = jnp.maximum(jnp.maximum(ls[0], ls[1]), ls[2])
    es = [jnp.exp(l - mx) for l in ls]
    den = es[0] + es[1] + es[2]
    os_ = [o0_ref[...], o1_ref[...], o2_ref[...]]
    mix = sum((e / den) * o.astype(F32) for e, o in zip(es, os_))
    out_ref[...] = x_ref[...] + _dot(mix.astype(BF16), w_ref[...])


def _dil_out(os_, ls, x, w):
    m = x.shape[0]
    tm = _row_tile(m)
    spec = pl.BlockSpec((tm, D_MODEL), lambda i: (i, 0))
    return pl.pallas_call(
        _dil_out_kernel,
        grid=(m // tm,),
        in_specs=[spec] * 7 + [_resident((D_MODEL, D_MODEL))],
        out_specs=spec,
        out_shape=jax.ShapeDtypeStruct((m, D_MODEL), F32),
        compiler_params=_cparams("parallel"),
        name="dil_out",
    )(*os_, *ls, x, w)


WIN_HEADS = 4


def _win_dec_kernel(qt_ref, knt_ref, vnt_ref, k0_ref, v0_ref, k1_ref, v1_ref, k2_ref, v2_ref,
                    o_ref, nk0_ref, nv0_ref, nk1_ref, nv1_ref, nk2_ref, nv2_ref):
    b, hc = pl.program_id(0), pl.program_id(1)
    rows = WIN_HEADS * HEAD_DIM
    cshape = (WIN_HEADS, HEAD_DIM, 1)
    caches = ((k0_ref, v0_ref, nk0_ref, nv0_ref), (k1_ref, v1_ref, nk1_ref, nv1_ref),
              (k2_ref, v2_ref, nk2_ref, nv2_ref))
    cols, ss, sn = [], [], []
    for g, (kc, _, _, _) in enumerate(caches):
        win, dil = DIL_PAIRS[g]
        sl = pl.ds(pl.multiple_of(g * D_MODEL + hc * rows, rows), rows)
        qc = _batch_column(qt_ref, b, sl).reshape(cshape)
        knc = _batch_column(knt_ref, b, sl).reshape(cshape)
        vnc = _batch_column(vnt_ref, b, sl).reshape(cshape)
        cols.append((knc, vnc))
        s = jnp.sum(kc[0].reshape(WIN_HEADS, HEAD_DIM, win) * qc, axis=1, keepdims=True)
        r = lax.broadcasted_iota(jnp.int32, s.shape, 2)
        ss.append(jnp.where((r & (dil - 1)) == 0, s, -jnp.inf))
        sn.append(jnp.sum(qc * knc, axis=1, keepdims=True))
    mx = sn[0]
    for g in range(N_DIL):
        mx = jnp.maximum(mx, jnp.maximum(sn[g], jnp.max(ss[g], axis=2, keepdims=True)))
    den = jnp.zeros((WIN_HEADS, 1, 1), F32)
    num = jnp.zeros(cshape, F32)
    for g, (kc, vc, nk, nv) in enumerate(caches):
        win, _ = DIL_PAIRS[g]
        knc, vnc = cols[g]
        p = jnp.exp(ss[g] - mx)
        pn = jnp.exp(sn[g] - mx)
        den = den + jnp.sum(p, axis=2, keepdims=True) + pn
        num = num + jnp.sum(p * vc[0].reshape(WIN_HEADS, HEAD_DIM, win), axis=2, keepdims=True) + pn * vnc
        last = lax.broadcasted_iota(jnp.int32, (rows, win), 1) == win - 1
        nk[0] = jnp.where(last, knc.reshape(rows, 1), pltpu.roll(kc[0], win - 1, 1))
        nv[0] = jnp.where(last, vnc.reshape(rows, 1), pltpu.roll(vc[0], win - 1, 1))
    o_ref[0] = jnp.broadcast_to(num / den, (WIN_HEADS, HEAD_DIM, LANE)).reshape(rows, LANE)


def _win_dec(qt, knt, vnt, caches):
    b = qt.shape[1]
    rows = WIN_HEADS * HEAD_DIM
    vec = pl.BlockSpec((N_DIL * D_MODEL, b), lambda i, hc: (0, 0))
    cspecs, cshapes = [], []
    for win, _ in DIL_PAIRS:
        cspecs += [pl.BlockSpec((1, rows, win), lambda i, hc: (i, hc, 0))] * 2
        cshapes += [jax.ShapeDtypeStruct((b, D_MODEL, win), F32)] * 2
    outs = pl.pallas_call(
        _win_dec_kernel,
        grid=(b, D_MODEL // rows),
        in_specs=[vec, vec, vec] + cspecs,
        out_specs=[pl.BlockSpec((1, rows, LANE), lambda i, hc: (i, hc, 0))] + cspecs,
        out_shape=[jax.ShapeDtypeStruct((b, D_MODEL, LANE), F32)] + cshapes,
        compiler_params=_cparams("parallel", "parallel"),
        name="win_dec",
    )(qt, knt, vnt, *caches)
    return outs[0], outs[1:]


def _prep_weights(W):
    P = {}
    row = lambda a: a.reshape(1, -1)
    for i in range(4):
        wup = W["f_w_up"][i].astype(BF16)
        wcv = jnp.pad(W["f_w_conv"][i], ((0, SUBLANE - 3), (0, 0)))
        wdn = W["f_w_down"][i].astype(BF16)
        P[f"ffn{i}"] = (row(W["norm_ffn"][i]), wup, wcv, wdn, row(W["norm_ple"][i]),
                        W["ple_w_gate"][i].astype(BF16), W["ple_w_proj"][i].astype(BF16), row(W["norm_final"]))
    P["norm_mix"] = [row(W["norm_mix"][i]) for i in range(4)]
    P["a_w_in"] = W["a_w_in"][0].astype(BF16)
    P["a_b_in"] = row(W["a_b_in"][0])
    P["a_w_dw"] = jnp.pad(W["a_w_dw"][0], ((0, CONV_HALO - CONV_A_WIDTH), (0, 0)))
    P["a_b_dw"] = row(W["a_b_dw"][0])
    P["a_ln_g"] = row(W["a_ln_g"][0])
    P["a_ln_b"] = row(W["a_ln_b"][0])
    P["a_w_out"] = W["a_w_out"][0].astype(BF16)
    P["b_w_qkv"] = W["b_w_qkv"][0].astype(BF16)
    P["b_w_o"] = W["b_w_o"][0].astype(BF16)
    P["s5"] = _s5_params(W["c_a_re"][0], W["c_a_im"][0], W["c_log_dt"][0], W["c_b_re"][0], W["c_b_im"][0],
                         W["c_c_re"][0], W["c_c_im"][0])
    P["c_d"] = row(W["c_d"][0])
    P["c_w_glu"] = W["c_w_glu"][0].astype(BF16)
    P["d_w_qkv"] = W["d_w_qkv"][0].astype(BF16)
    P["d_w_o"] = W["d_w_o"][0].astype(BF16)
    return P


def _heads(a, batch, rows):
    return a.reshape(1, batch, rows, N_HEADS, HEAD_DIM)


def _heads_t(a, batch, rows):
    return a.reshape(batch, N_HEADS, HEAD_DIM, rows).transpose(0, 3, 1, 2)[None]


def _cache_t(c):
    n, rows = c.shape[0], c.shape[1]
    return c.transpose(0, 2, 3, 1).reshape(n, D_MODEL, rows)


def _prompt_trunk(x, p, P):
    batch, seq, _ = x.shape
    m = batch * seq
    x = x.reshape(m, D_MODEL)
    tabs = _rope_tables(jnp.arange(seq, dtype=jnp.int32))
    st = {}
    u = _conv_in(x, P["norm_mix"][0], P["a_w_in"], P["a_b_in"])
    st["conv_a"] = u.reshape(batch, seq, D_MODEL)[None, :, seq - (CONV_A_WIDTH - 1):]
    x = _conv_out(u, x, seq, P["a_w_dw"], P["a_b_dw"], P["a_ln_g"], P["a_ln_b"], P["a_w_out"])
    ffn_states = []
    x, fs = _ffn_prompt(x, p[0].reshape(m, PLE_DIM), P["ffn0"], batch, seq, False)
    ffn_states.append(fs)
    q, k, v, kt, vt = _qkv(x, P["norm_mix"][1], P["b_w_qkv"], tabs, 1, seq, batch)
    st["moba_k"] = _heads_t(kt, batch, seq)
    st["moba_v"] = _heads_t(vt, batch, seq)
    o = _moba_attn(q[0].reshape(batch, seq, D_MODEL), k[0].reshape(batch, seq, D_MODEL),
                   v[0].reshape(batch, seq, D_MODEL), batch, seq)
    x = _proj(o.reshape(m, D_MODEL), x, P["b_w_o"])
    x, fs = _ffn_prompt(x, p[1].reshape(m, PLE_DIM), P["ffn1"], batch, seq, False)
    ffn_states.append(fs)
    s5 = P["s5"]
    bu_re, bu_im = _s5_in(x, P["norm_mix"][2], s5["wre"], s5["wim"])
    xre, xim = _s5_scan(bu_re, bu_im, s5["abar_re"], s5["abar_im"], batch, seq)
    st["s5_re"] = xre.reshape(batch, seq, S5_GROUPS, S5_STATE)[None, :, seq - 1]
    st["s5_im"] = xim.reshape(batch, seq, S5_GROUPS, S5_STATE)[None, :, seq - 1]
    x = _s5_out(x, P["norm_mix"][2], xre, xim, s5["cre"], s5["cim"], P["c_d"], P["c_w_glu"])
    x, fs = _ffn_prompt(x, p[2].reshape(m, PLE_DIM), P["ffn2"], batch, seq, False)
    ffn_states.append(fs)
    q, k, v, kt, vt = _qkv(x, P["norm_mix"][3], P["d_w_qkv"], tabs, N_DIL, seq, batch)
    os_, ls = [], []
    for g, (win, dil) in enumerate(DIL_PAIRS):
        keep = min(win, seq)
        feat = slice(g * D_MODEL, (g + 1) * D_MODEL)
        st[f"win_k{g + 1}"] = _heads_t(kt[:, feat, seq - keep:], batch, keep)
        st[f"win_v{g + 1}"] = _heads_t(vt[:, feat, seq - keep:], batch, keep)
        o, lse = _dil_attn(q[g], k[g], v[g], g, dil, batch, seq)
        os_.append(o)
        ls.append(lse)
    x = _dil_out(os_, ls, x, P["d_w_o"])
    x, fs = _ffn_prompt(x, p[3].reshape(m, PLE_DIM), P["ffn3"], batch, seq, True)
    ffn_states.append(fs)
    st["ffn_conv"] = jnp.stack(ffn_states, 0)
    return x.reshape(batch, seq, D_MODEL), st


def _sample_trunk(x, p, P, S, past_len):
    b = x.shape[0]
    x = x.reshape(b, D_MODEL)
    tabs = _rope_tables(jnp.full((b,), past_len, dtype=jnp.int32))
    st = {}
    buf = S["conv_a"][0]
    x, u = _conv_dec(x, P["norm_mix"][0], P["a_w_in"], P["a_b_in"], buf.transpose(1, 0, 2), P["a_w_dw"],
                     P["a_b_dw"], P["a_ln_g"], P["a_ln_b"], P["a_w_out"])
    st["conv_a"] = jnp.concatenate([buf[:, 1:], u[:, None]], axis=1)[None]
    ffn_states = []
    x, fs = _ffn_dec(x, p[0].reshape(b, PLE_DIM), P["ffn0"], S["ffn_conv"][0], False)
    ffn_states.append(fs)
    q, k, v = _qkv(x, P["norm_mix"][1], P["b_w_qkv"], tabs, 1, b)
    st["moba_k"] = _heads(k, b, 1)
    st["moba_v"] = _heads(v, b, 1)
    o = _moba_dec(q.astype(F32).T, k.T, v.T, _cache_t(S["moba_k"][0]), _cache_t(S["moba_v"][0]),
                  S["page_table"])
    x = _proj(o[:, :, 0].astype(BF16), x, P["b_w_o"])
    x, fs = _ffn_dec(x, p[1].reshape(b, PLE_DIM), P["ffn1"], S["ffn_conv"][1], False)
    ffn_states.append(fs)
    s5 = P["s5"]
    bu_re, bu_im = _s5_in(x, P["norm_mix"][2], s5["wre"], s5["wim"])
    xre, xim = _s5_step(bu_re, bu_im, S["s5_re"][0].reshape(b, S5_DIM), S["s5_im"][0].reshape(b, S5_DIM),
                        s5["abar_re"], s5["abar_im"])
    st["s5_re"] = xre.reshape(1, b, S5_GROUPS, S5_STATE)
    st["s5_im"] = xim.reshape(1, b, S5_GROUPS, S5_STATE)
    x = _s5_out(x, P["norm_mix"][2], xre, xim, s5["cre"], s5["cim"], P["c_d"], P["c_w_glu"])
    x, fs = _ffn_dec(x, p[2].reshape(b, PLE_DIM), P["ffn2"], S["ffn_conv"][2], False)
    ffn_states.append(fs)
    q, k, v = _qkv(x, P["norm_mix"][3], P["d_w_qkv"], tabs, N_DIL, b)
    caches = []
    for g in range(N_DIL):
        caches += [_cache_t(S[f"win_k{g + 1}"][0]), _cache_t(S[f"win_v{g + 1}"][0])]
    o, new = _win_dec(q.astype(F32).T, k.T, v.T, caches)
    for g, (win, _) in enumerate(DIL_PAIRS):
        st[f"win_k{g + 1}"] = _heads_t(new[2 * g], b, win)
        st[f"win_v{g + 1}"] = _heads_t(new[2 * g + 1], b, win)
    x = _proj(o[:, :, 0].astype(BF16), x, P["d_w_o"])
    x, fs = _ffn_dec(x, p[3].reshape(b, PLE_DIM), P["ffn3"], S["ffn_conv"][3], True)
    ffn_states.append(fs)
    st["ffn_conv"] = jnp.stack(ffn_states, 0)
    return x.reshape(b, 1, D_MODEL), st


_STATE_KEYS = ("conv_a", "moba_k", "moba_v", "s5_re", "s5_im", "win_k1", "win_v1", "win_k2", "win_v2",
               "win_k3", "win_v3", "ffn_conv")


def kernel(x_prompt, x_sample, state_conv_a, cache_moba_k, cache_moba_v, state_s5_re, state_s5_im, cache_win_k1, cache_win_v1, cache_win_k2, cache_win_v2, cache_win_k3, cache_win_v3, state_ffn_conv, page_table, p_prompt, p_sample, norm_mix, norm_ffn, norm_ple, norm_final, a_w_in, a_b_in, a_w_dw, a_b_dw, a_ln_g, a_ln_b, a_w_out, b_w_qkv, b_w_o, c_a_re, c_a_im, c_log_dt, c_b_re, c_b_im, c_c_re, c_c_im, c_d, c_w_glu, d_w_qkv, d_w_o, f_w_up, f_w_conv, f_w_down, ple_w_proj, ple_w_gate):
    W = dict(norm_mix=norm_mix, norm_ffn=norm_ffn, norm_ple=norm_ple, norm_final=norm_final,
             a_w_in=a_w_in, a_b_in=a_b_in, a_w_dw=a_w_dw, a_b_dw=a_b_dw, a_ln_g=a_ln_g, a_ln_b=a_ln_b,
             a_w_out=a_w_out, b_w_qkv=b_w_qkv, b_w_o=b_w_o,
             c_a_re=c_a_re, c_a_im=c_a_im, c_log_dt=c_log_dt, c_b_re=c_b_re, c_b_im=c_b_im,
             c_c_re=c_c_re, c_c_im=c_c_im, c_d=c_d, c_w_glu=c_w_glu,
             d_w_qkv=d_w_qkv, d_w_o=d_w_o, f_w_up=f_w_up, f_w_conv=f_w_conv, f_w_down=f_w_down,
             ple_w_proj=ple_w_proj, ple_w_gate=ple_w_gate)
    S = dict(conv_a=state_conv_a, moba_k=cache_moba_k, moba_v=cache_moba_v, s5_re=state_s5_re, s5_im=state_s5_im,
             win_k1=cache_win_k1, win_v1=cache_win_v1, win_k2=cache_win_k2, win_v2=cache_win_v2,
             win_k3=cache_win_k3, win_v3=cache_win_v3, ffn_conv=state_ffn_conv, page_table=page_table)
    P = _prep_weights(W)
    past_len = page_table.shape[1] * PAGE_SIZE
    y_prompt, sp = _prompt_trunk(x_prompt, p_prompt, P)
    y_sample, ss = _sample_trunk(x_sample, p_sample, P, S, past_len)
    return (y_prompt, y_sample) + tuple(sp[n] for n in _STATE_KEYS) + tuple(ss[n] for n in _STATE_KEYS)
```

```python
import functools
import math

import jax
import jax.numpy as jnp
from jax import lax
from jax.experimental import pallas as pl
from jax.experimental.pallas import tpu as pltpu

F32 = jnp.float32
BF16 = jnp.bfloat16

D_MODEL = 1024
N_HEADS = 16
HEAD_DIM = 64
ROT_DIM = 16
ROPE_THETA = 500000.0
EPS = 1e-6
PLE_DIM = 256
CONV_A_WIDTH = 31
MOBA_BLOCK = 256
MOBA_TOPK = 3
S5_GROUP = 16
S5_GROUPS = 64
S5_STATE = 64
S5_DIM = S5_GROUPS * S5_STATE
DIL_PAIRS = ((128, 1), (512, 4), (2048, 16))
N_DIL = 3
DIL_TAPS = 128
D_FF = 2816
PAGE_SIZE = 128

LANE = 128
SUBLANE = 8
FF_CHUNK = 256
N_FF_CHUNK = D_FF // FF_CHUNK
VMEM_LIMIT = 56 * 1024 * 1024
MASK_NEG = -30000.0

_NT = (((1,), (1,)), ((), ()))


def _cparams(*sem):
    return pltpu.CompilerParams(dimension_semantics=sem, vmem_limit_bytes=VMEM_LIMIT)


def _resident(shape):
    nd = len(shape)
    return pl.BlockSpec(shape, lambda *_: (0,) * nd, pipeline_mode=pl.Buffered(1))


def _rms(x, g):
    return x * lax.rsqrt(jnp.mean(x * x, axis=-1, keepdims=True) + EPS) * g


def _dot(a, b):
    return jnp.dot(a, b, preferred_element_type=F32)


def _row_tile(m):
    return 512 if m % 512 == 0 else m


def _conv_in_kernel(x_ref, g_ref, w_ref, b_ref, u_ref):
    h = _rms(x_ref[...], g_ref[...]).astype(BF16)
    z = _dot(h, w_ref[...]) + b_ref[...]
    u_ref[...] = z[:, :D_MODEL] * jax.nn.sigmoid(z[:, D_MODEL:])


def _conv_in(x, g, w, b):
    m = x.shape[0]
    tm = _row_tile(m)
    return pl.pallas_call(
        _conv_in_kernel,
        grid=(m // tm,),
        in_specs=[pl.BlockSpec((tm, D_MODEL), lambda i: (i, 0)),
                  _resident((1, D_MODEL)), _resident((D_MODEL, 2 * D_MODEL)), _resident((1, 2 * D_MODEL))],
        out_specs=pl.BlockSpec((tm, D_MODEL), lambda i: (i, 0)),
        out_shape=jax.ShapeDtypeStruct((m, D_MODEL), F32),
        compiler_params=_cparams("parallel"),
        name="conv_in",
    )(x, g, w, b)


def _ln_silu_out(cf, x, lng, lnb, wout):
    mu = jnp.mean(cf, axis=-1, keepdims=True)
    var = jnp.mean(jnp.square(cf - mu), axis=-1, keepdims=True)
    cn = (cf - mu) * lax.rsqrt(var + EPS) * lng + lnb
    return x + _dot(jax.nn.silu(cn).astype(BF16), wout)


CONV_HALO = 32


def _conv_out_kernel(u_ref, halo_ref, x_ref, wdw_ref, bdw_ref, lng_ref, lnb_ref, wout_ref, o_ref,
                     ext_ref, cf_ref, *, tm, tiles_per_seq):
    first = (pl.program_id(0) % tiles_per_seq) == 0
    ext_ref[0:CONV_HALO, :] = jnp.where(first, 0.0, halo_ref[...])
    ext_ref[CONV_HALO:, :] = u_ref[...]
    off = CONV_HALO - (CONV_A_WIDTH - 1)
    for c in range(D_MODEL // LANE):
        cols = slice(c * LANE, (c + 1) * LANE)
        acc = jnp.zeros((tm, LANE), F32)
        for j in range(CONV_A_WIDTH):
            acc = acc + wdw_ref[j:j + 1, cols] * ext_ref[pl.ds(off + j, tm), cols]
        cf_ref[:, cols] = acc + bdw_ref[:, cols]
    o_ref[...] = _ln_silu_out(cf_ref[...], x_ref[...], lng_ref[...], lnb_ref[...], wout_ref[...])


def _conv_out(u, x, seq, wdw, bdw, lng, lnb, wout):
    m = x.shape[0]
    tm = 256
    tps = seq // tm
    hb = tm // CONV_HALO
    kern = functools.partial(_conv_out_kernel, tm=tm, tiles_per_seq=tps)
    return pl.pallas_call(
        kern,
        grid=(m // tm,),
        in_specs=[pl.BlockSpec((tm, D_MODEL), lambda i: (i, 0)),
                  pl.BlockSpec((CONV_HALO, D_MODEL), lambda i: (jnp.maximum(i * hb - 1, 0), 0)),
                  pl.BlockSpec((tm, D_MODEL), lambda i: (i, 0)),
                  _resident((CONV_HALO, D_MODEL)), _resident((1, D_MODEL)), _resident((1, D_MODEL)),
                  _resident((1, D_MODEL)), _resident((D_MODEL, D_MODEL))],
        out_specs=pl.BlockSpec((tm, D_MODEL), lambda i: (i, 0)),
        out_shape=jax.ShapeDtypeStruct((m, D_MODEL), F32),
        scratch_shapes=[pltpu.VMEM((tm + CONV_HALO, D_MODEL), F32), pltpu.VMEM((tm, D_MODEL), F32)],
        compiler_params=_cparams("parallel"),
        name="conv_out",
    )(u, u, x, wdw, bdw, lng, lnb, wout)


def _conv_dec_kernel(x_ref, g_ref, win_ref, bin_ref, buf_ref, wdw_ref, bdw_ref, lng_ref, lnb_ref, wout_ref,
                     o_ref, u_ref, acc_ref):
    j = pl.program_id(0)

    @pl.when(j == 0)
    def _():
        h = _rms(x_ref[...], g_ref[...]).astype(BF16)
        z = _dot(h, win_ref[...]) + bin_ref[...]
        u = z[:, :D_MODEL] * jax.nn.sigmoid(z[:, D_MODEL:])
        u_ref[...] = u
        acc_ref[...] = wdw_ref[CONV_A_WIDTH - 1:CONV_A_WIDTH, :] * u

    acc_ref[...] += wdw_ref[pl.ds(j, 1), :] * buf_ref[0]

    @pl.when(j == CONV_A_WIDTH - 2)
    def _():
        cf = acc_ref[...] + bdw_ref[...]
        o_ref[...] = _ln_silu_out(cf, x_ref[...], lng_ref[...], lnb_ref[...], wout_ref[...])


def _conv_dec(x, g, win, bin_, buf_t, wdw, bdw, lng, lnb, wout):
    m = x.shape[0]
    return pl.pallas_call(
        _conv_dec_kernel,
        grid=(CONV_A_WIDTH - 1,),
        in_specs=[_resident((m, D_MODEL)), _resident((1, D_MODEL)), _resident((D_MODEL, 2 * D_MODEL)),
                  _resident((1, 2 * D_MODEL)),
                  pl.BlockSpec((1, m, D_MODEL), lambda j: (j, 0, 0)),
                  _resident((CONV_HALO, D_MODEL)), _resident((1, D_MODEL)), _resident((1, D_MODEL)),
                  _resident((1, D_MODEL)), _resident((D_MODEL, D_MODEL))],
        out_specs=[pl.BlockSpec((m, D_MODEL), lambda j: (0, 0)), pl.BlockSpec((m, D_MODEL), lambda j: (0, 0))],
        out_shape=[jax.ShapeDtypeStruct((m, D_MODEL), F32), jax.ShapeDtypeStruct((m, D_MODEL), F32)],
        scratch_shapes=[pltpu.VMEM((m, D_MODEL), F32)],
        compiler_params=_cparams("arbitrary"),
        name="conv_dec",
    )(x, g, win, bin_, buf_t, wdw, bdw, lng, lnb, wout)


def _ffn_tail(x2, p_ref, gple_ref, wgate_ref, wproj_ref, gfin_ref, final):
    hn = _rms(x2, gple_ref[...]).astype(BF16)
    gate = jax.nn.sigmoid(_dot(hn, wgate_ref[...]))
    x3 = x2 + _dot(p_ref[...].astype(BF16), wproj_ref[...]) * gate
    if final:
        x3 = _rms(x3, gfin_ref[...])
    return x3


def _ffn_prompt_kernel(x_ref, p_ref, gffn_ref, wup_ref, wcv_ref, wdn_ref, gple_ref, wgate_ref, wproj_ref,
                       gfin_ref, o_ref, st_ref, h_ref, act_ref, ua_ref, ub_ref, carry_ref,
                       *, tm, tiles_per_seq, final):
    t = pl.program_id(0) % tiles_per_seq

    @pl.when(t == 0)
    def _():
        carry_ref[...] = jnp.zeros_like(carry_ref)

    h_ref[...] = _rms(x_ref[...], gffn_ref[...]).astype(BF16)

    def conv_half(c, ubuf):
        u = _dot(h_ref[...], wup_ref[:, _ff_cols(c)])
        ubuf[0:SUBLANE, :] = carry_ref[c]
        ubuf[SUBLANE:, :] = u
        w = wcv_ref[:, _ff_cols(c)]
        cc = (w[0:1] * ubuf[pl.ds(SUBLANE - 2, tm), :] + w[1:2] * ubuf[pl.ds(SUBLANE - 1, tm), :]
              + w[2:3] * u)
        carry_ref[c] = ubuf[pl.ds(tm, SUBLANE), :]
        return cc

    def chunk(j, slot):
        ca = conv_half(j, ua_ref.at[slot])
        cb = conv_half(j + N_FF_CHUNK, ub_ref.at[slot])
        act_ref[j] = (jax.nn.silu(ca) * cb).astype(BF16)

    for j in range(N_FF_CHUNK):
        chunk(j, j % 2)

    @pl.when(t == tiles_per_seq - 1)
    def _():
        st_ref[0] = carry_ref[...]

    y = _dot(act_ref[0], wdn_ref[_ff_rows(0), :])
    for j in range(1, N_FF_CHUNK):
        y = y + _dot(act_ref[j], wdn_ref[_ff_rows(j), :])
    o_ref[...] = _ffn_tail(x_ref[...] + y, p_ref, gple_ref, wgate_ref, wproj_ref, gfin_ref, final)


def _ffn_dec_kernel(x_ref, p_ref, gffn_ref, wup_ref, wcv_ref, wdn_ref, gple_ref, wgate_ref, wproj_ref,
                    gfin_ref, s0_ref, s1_ref, o_ref, u_ref, h_ref, *, final):
    h_ref[...] = _rms(x_ref[...], gffn_ref[...]).astype(BF16)

    def conv_half(c):
        u = _dot(h_ref[...], wup_ref[:, _ff_cols(c)])
        u_ref[c] = u
        w = wcv_ref[:, _ff_cols(c)]
        return w[0:1] * s0_ref[c] + w[1:2] * s1_ref[c] + w[2:3] * u

    y = None
    for j in range(N_FF_CHUNK):
        act = (jax.nn.silu(conv_half(j)) * conv_half(j + N_FF_CHUNK)).astype(BF16)
        d = _dot(act, wdn_ref[_ff_rows(j), :])
        y = d if y is None else y + d
    o_ref[...] = _ffn_tail(x_ref[...] + y, p_ref, gple_ref, wgate_ref, wproj_ref, gfin_ref, final)


def _ff_cols(c):
    return slice(c * FF_CHUNK, (c + 1) * FF_CHUNK)


_ff_rows = _ff_cols


def _ffn_weight_specs():
    return [_resident((1, D_MODEL)),
            _resident((D_MODEL, 2 * D_FF)),
            _resident((SUBLANE, 2 * D_FF)),
            _resident((D_FF, D_MODEL)),
            _resident((1, D_MODEL)), _resident((D_MODEL, D_MODEL)), _resident((PLE_DIM, D_MODEL)),
            _resident((1, D_MODEL))]


def _ffn_prompt(x, p, wts, batch, seq, final):
    m = x.shape[0]
    tm = _row_tile(seq)
    tps = seq // tm
    kern = functools.partial(_ffn_prompt_kernel, tm=tm, tiles_per_seq=tps, final=final)
    nc = 2 * N_FF_CHUNK
    out, st = pl.pallas_call(
        kern,
        grid=(m // tm,),
        in_specs=[pl.BlockSpec((tm, D_MODEL), lambda i: (i, 0)),
                  pl.BlockSpec((tm, PLE_DIM), lambda i: (i, 0))] + _ffn_weight_specs(),
        out_specs=[pl.BlockSpec((tm, D_MODEL), lambda i: (i, 0)),
                   pl.BlockSpec((1, nc, SUBLANE, FF_CHUNK), lambda i: (i // tps, 0, 0, 0))],
        out_shape=[jax.ShapeDtypeStruct((m, D_MODEL), F32),
                   jax.ShapeDtypeStruct((batch, nc, SUBLANE, FF_CHUNK), F32)],
        scratch_shapes=[pltpu.VMEM((tm, D_MODEL), BF16), pltpu.VMEM((N_FF_CHUNK, tm, FF_CHUNK), BF16),
                        pltpu.VMEM((2, tm + SUBLANE, FF_CHUNK), F32), pltpu.VMEM((2, tm + SUBLANE, FF_CHUNK), F32),
                        pltpu.VMEM((nc, SUBLANE, FF_CHUNK), F32)],
        compiler_params=_cparams("arbitrary"),
        name="ffn_prompt",
    )(x, p, *wts)
    new_state = st[:, :, SUBLANE - 2:, :].transpose(0, 2, 1, 3).reshape(batch, 2, 2 * D_FF)
    return out, new_state


def _ffn_dec(x, p, wts, state, final):
    m = x.shape[0]
    nc = 2 * N_FF_CHUNK
    kern = functools.partial(_ffn_dec_kernel, final=final)
    s = state.reshape(m, 2, nc, FF_CHUNK).transpose(1, 2, 0, 3)
    out, u = pl.pallas_call(
        kern,
        grid=(1,),
        in_specs=[_resident((m, D_MODEL)), _resident((m, PLE_DIM))] + _ffn_weight_specs()
        + [_resident((nc, m, FF_CHUNK)), _resident((nc, m, FF_CHUNK))],
        out_specs=[pl.BlockSpec((m, D_MODEL), lambda i: (0, 0)),
                   pl.BlockSpec((nc, m, FF_CHUNK), lambda i: (0, 0, 0))],
        out_shape=[jax.ShapeDtypeStruct((m, D_MODEL), F32), jax.ShapeDtypeStruct((nc, m, FF_CHUNK), F32)],
        scratch_shapes=[pltpu.VMEM((m, D_MODEL), BF16)],
        compiler_params=_cparams("arbitrary"),
        name="ffn_dec",
    )(x, p, *wts, s[0], s[1])
    u_rows = u.transpose(1, 0, 2).reshape(m, 1, 2 * D_FF)
    return out, jnp.concatenate([state[:, 1:], u_rows], axis=1)


def _qkv_kernel(x_ref, g_ref, w_ref, c_ref, s1_ref, s2_ref, q_ref, k_ref, v_ref, *t_refs):
    h = _rms(x_ref[...], g_ref[...]).astype(BF16)
    reps = D_MODEL // LANE
    cos = jnp.concatenate([c_ref[...]] * reps, axis=1)
    sn1 = jnp.concatenate([s1_ref[...]] * reps, axis=1)
    sn2 = jnp.concatenate([s2_ref[...]] * reps, axis=1)
    half = ROT_DIM // 2

    def rope(z):
        return (z * cos + pltpu.roll(z, D_MODEL - half, 1) * sn1 + pltpu.roll(z, half, 1) * sn2)

    q = rope(_dot(h, w_ref[:, 0:D_MODEL]))
    q_ref[...] = (q * (HEAD_DIM ** -0.5)).astype(BF16)
    k = rope(_dot(h, w_ref[:, D_MODEL:2 * D_MODEL]))
    v = _dot(h, w_ref[:, 2 * D_MODEL:])
    k_ref[...] = k.astype(k_ref.dtype)
    v_ref[...] = v.astype(v_ref.dtype)
    if t_refs:
        kt_ref, vt_ref = t_refs
        kt_ref[0] = k.T
        vt_ref[0] = v.T


def _qkv(x, g, w, tabs, n_groups, seq, batch=None):
    m = x.shape[0]
    tm = _row_tile(seq)
    npb = seq // tm
    row = lambda gi, i: (i, 0)
    col = lambda gi, i: (i, gi)
    tab = pl.BlockSpec((tm, LANE), lambda gi, i: (i % npb, 0))
    n = n_groups * D_MODEL
    if batch is None:
        out_specs = [pl.BlockSpec((tm, D_MODEL), col)] * 3
        out_shape = [jax.ShapeDtypeStruct((m, n), BF16)] + [jax.ShapeDtypeStruct((m, n), F32)] * 2
    else:
        out_specs = [pl.BlockSpec((None, tm, D_MODEL), lambda gi, i: (gi, i, 0))] * 3
        out_shape = [jax.ShapeDtypeStruct((n_groups, m, D_MODEL), BF16)] * 3
        out_specs += [pl.BlockSpec((1, D_MODEL, tm), lambda gi, i: (i // npb, gi, i % npb))] * 2
        out_shape += [jax.ShapeDtypeStruct((batch, n, seq), F32)] * 2
    return pl.pallas_call(
        _qkv_kernel,
        grid=(n_groups, m // tm),
        in_specs=[pl.BlockSpec((tm, D_MODEL), row), pl.BlockSpec((1, D_MODEL), lambda gi, i: (0, 0)),
                  pl.BlockSpec((D_MODEL, 3 * D_MODEL), lambda gi, i: (0, gi)), tab, tab, tab],
        out_specs=out_specs,
        out_shape=out_shape,
        compiler_params=_cparams("parallel", "parallel"),
        name="qkv_rope",
    )(x, g, w, *tabs)


def _rope_tables(pos):
    half = ROT_DIM // 2
    inv_freq = ROPE_THETA ** (-jnp.arange(half, dtype=F32) / half)
    ang = pos.astype(F32)[:, None] * inv_freq[None, :]
    cos, sin = jnp.cos(ang), jnp.sin(ang)
    lane = jnp.arange(LANE) % HEAD_DIM
    idx = lane % half
    c = jnp.where(lane[None, :] < ROT_DIM, cos[:, idx], 1.0)
    s1 = jnp.where(lane[None, :] < half, -sin[:, idx], 0.0)
    s2 = jnp.where((lane[None, :] >= half) & (lane[None, :] < ROT_DIM), sin[:, idx], 0.0)
    return c, s1, s2


def _proj_kernel(a_ref, x_ref, w_ref, o_ref):
    o_ref[...] = x_ref[...] + _dot(a_ref[...], w_ref[...])


def _proj(a, x, w):
    m = x.shape[0]
    tm = _row_tile(m)
    return pl.pallas_call(
        _proj_kernel,
        grid=(m // tm,),
        in_specs=[pl.BlockSpec((tm, D_MODEL), lambda i: (i, 0)), pl.BlockSpec((tm, D_MODEL), lambda i: (i, 0)),
                  _resident((D_MODEL, D_MODEL))],
        out_specs=pl.BlockSpec((tm, D_MODEL), lambda i: (i, 0)),
        out_shape=jax.ShapeDtypeStruct((m, D_MODEL), F32),
        compiler_params=_cparams("parallel"),
        name="out_proj",
    )(a, x, w)


def _moba_kernel(q_ref, k_ref, v_ref, o_ref, ka0_ref, ka1_ref, va0_ref, va1_ref, km_ref, *, seq):
    nb = seq // MOBA_BLOCK
    qi = pl.program_id(2)
    blk = MOBA_BLOCK
    span = 2 * blk

    @pl.when(qi == 0)
    def _():
        kf = k_ref[0].astype(F32)
        rblk = lax.broadcasted_iota(jnp.int32, (seq, LANE), 0) >> int(math.log2(blk))
        lane = lax.broadcasted_iota(jnp.int32, (seq, LANE), 1)
        ka0_ref[...] = jnp.where(lane < HEAD_DIM, kf, (lane - HEAD_DIM == rblk).astype(F32)).astype(BF16)
        ka1_ref[...] = jnp.where(lane >= HEAD_DIM, kf, (lane == rblk).astype(F32)).astype(BF16)
        km_ref[...] = jnp.mean(kf.reshape(nb, blk, LANE), axis=1)
        vf = v_ref[0].astype(F32)
        va0_ref[...] = jnp.where(lane < HEAD_DIM, vf, 1.0).astype(BF16)
        va1_ref[...] = jnp.where(lane >= HEAD_DIM, vf, 1.0).astype(BF16)

    q2 = q_ref[0]
    lane_q = lax.broadcasted_iota(jnp.int32, (blk, LANE), 1)
    lane_k = lax.broadcasted_iota(jnp.int32, (nb, LANE), 1)
    n_iota = lax.broadcasted_iota(jnp.int32, (nb, blk), 0)
    row = lax.broadcasted_iota(jnp.int32, (blk, blk), 0)
    colk = lax.broadcasted_iota(jnp.int32, (blk, blk), 1)
    km = km_ref[...]
    start = pl.multiple_of(qi * blk, blk)
    qas, init = [], []
    heads = ((ka0_ref, va0_ref), (ka1_ref, va1_ref))
    for hd, (ka_ref, va_ref) in enumerate(heads):
        mine_q = (lane_q < HEAD_DIM) if hd == 0 else (lane_q >= HEAD_DIM)
        mine_k = (lane_k < HEAD_DIM) if hd == 0 else (lane_k >= HEAD_DIM)
        kmh = jnp.where(mine_k, km, 0.0)
        hi = kmh.astype(BF16)
        lo = (kmh - hi.astype(F32)).astype(BF16)
        st = (lax.dot_general(hi, q2, _NT, preferred_element_type=F32)
              + lax.dot_general(lo, q2, _NT, preferred_element_type=F32))
        s = jnp.where(n_iota < qi, st, -jnp.inf)
        pen = jnp.full((nb, blk), MASK_NEG, F32)
        for _ in range(MOBA_TOPK):
            mx = jnp.max(s, axis=0, keepdims=True)
            idx = jnp.min(jnp.where(s == mx, n_iota, nb), axis=0, keepdims=True)
            chosen = jnp.logical_and(n_iota == idx, mx > -jnp.inf)
            pen = jnp.where(chosen, 0.0, pen)
            s = jnp.where(chosen, -jnp.inf, s)
        n_lo = HEAD_DIM if hd == 0 else 0
        parts = [jnp.zeros((n_lo, blk), F32)] if n_lo else []
        parts += [pen, jnp.zeros((LANE - nb - n_lo, blk), F32)]
        pen_q = jnp.concatenate(parts, axis=0).T
        qas.append(jnp.where(mine_q, q2, pen_q.astype(BF16)))

        qd = jnp.where(mine_q, q2, jnp.zeros_like(q2))
        s0 = lax.dot_general(qd, ka_ref[pl.ds(start, blk), :], _NT, preferred_element_type=F32)
        s0 = jnp.where(colk <= row, s0, -jnp.inf)
        m0 = jnp.max(s0, axis=1, keepdims=True)
        p0 = jnp.exp(s0 - m0)
        init += [m0, _dot(p0.astype(BF16), va_ref[pl.ds(start, blk), :])]

    def past(j, carry):
        off = pl.multiple_of(j * span, span)
        new = []
        for hd, (ka_ref, va_ref) in enumerate(heads):
            m, acc = carry[2 * hd:2 * hd + 2]
            sn = lax.dot_general(qas[hd], ka_ref[pl.ds(off, span), :], _NT, preferred_element_type=F32)
            mn = jnp.maximum(m, jnp.max(sn, axis=1, keepdims=True))
            p = jnp.exp(sn - mn)
            new += [mn, jnp.exp(m - mn) * acc + _dot(p.astype(BF16), va_ref[pl.ds(off, span), :])]
        return tuple(new)

    fin = lax.fori_loop(0, (qi + 1) // 2, past, tuple(init))
    outs = [acc / pltpu.roll(acc, HEAD_DIM, 1) for acc in (fin[1], fin[3])]
    o_ref[0] = jnp.where(lane_q < HEAD_DIM, outs[0], outs[1]).astype(BF16)


def _moba_attn(q, k, v, batch, seq):
    nhp = D_MODEL // LANE
    nq = seq // MOBA_BLOCK
    kern = functools.partial(_moba_kernel, seq=seq)
    qspec = pl.BlockSpec((1, MOBA_BLOCK, LANE), lambda b, hp, qi: (b, qi, hp))
    kvspec = pl.BlockSpec((1, seq, LANE), lambda b, hp, qi: (b, 0, hp))
    return pl.pallas_call(
        kern,
        grid=(batch, nhp, nq),
        in_specs=[qspec, kvspec, kvspec],
        out_specs=qspec,
        out_shape=jax.ShapeDtypeStruct((batch, seq, D_MODEL), BF16),
        scratch_shapes=[pltpu.VMEM((seq, LANE), BF16)] * 4 + [pltpu.VMEM((seq // MOBA_BLOCK, LANE), F32)],
        compiler_params=_cparams("parallel", "parallel", "arbitrary"),
        name="moba_attn",
    )(q, k, v)


def _batch_column(ref, b, rows=None):
    tile = ref[...] if rows is None else ref[rows, :]
    lane = lax.broadcasted_iota(jnp.int32, tile.shape, 1)
    return jnp.sum(jnp.where(lane == b, tile, 0.0), axis=1, keepdims=True)


MOBA_DEC_PAGES = 4


def _moba_dec_kernel(pt_ref, qt_ref, knt_ref, vnt_ref, *refs, n_blk):
    k_refs, v_refs = refs[:MOBA_DEC_PAGES], refs[MOBA_DEC_PAGES:2 * MOBA_DEC_PAGES]
    o_ref, qb_ref, s_ref, vall_ref = refs[2 * MOBA_DEC_PAGES:]
    b, n = pl.program_id(0), pl.program_id(1)
    hshape = (N_HEADS, HEAD_DIM, PAGE_SIZE)

    @pl.when(n == 0)
    def _():
        qb_ref[...] = jnp.broadcast_to(_batch_column(qt_ref, b), (D_MODEL, PAGE_SIZE))

    qb = qb_ref[...].reshape(hshape)
    for j, (kr, vr) in enumerate(zip(k_refs, v_refs)):
        pg = MOBA_DEC_PAGES * n + j
        s_ref[pg] = jnp.sum(kr[0].reshape(hshape) * qb, axis=1, keepdims=True)
        vall_ref[pg] = vr[0]

    @pl.when(n == pl.num_programs(1) - 1)
    def _():
        cshape = (N_HEADS, HEAD_DIM, 1)
        qc = qb_ref[:, 0:1].reshape(cshape)
        knc = _batch_column(knt_ref, b).reshape(cshape)
        vnc = _batch_column(vnt_ref, b).reshape(cshape)
        s_own = jnp.sum(qc * knc, axis=1, keepdims=True)
        bsum, bmax = [], []
        for i in range(n_blk):
            a, c = s_ref[2 * i], s_ref[2 * i + 1]
            bsum.append(jnp.sum(a + c, axis=2, keepdims=True))
            bmax.append(jnp.max(jnp.maximum(a, c), axis=2, keepdims=True))
        sels = []
        for i in range(n_blk):
            rank = jnp.zeros((N_HEADS, 1, 1), F32)
            for m in range(n_blk):
                if m != i:
                    beats = (bsum[m] >= bsum[i]) if m < i else (bsum[m] > bsum[i])
                    rank = rank + jnp.where(beats, 1.0, 0.0)
            sels.append(rank < MOBA_TOPK)
        mx = s_own
        for i in range(n_blk):
            mx = jnp.maximum(mx, jnp.where(sels[i], bmax[i], -jnp.inf))
        w_own = jnp.exp(s_own - mx)
        lacc = jnp.zeros((N_HEADS, 1, PAGE_SIZE), F32)
        vacc = jnp.zeros(hshape, F32)
        for pg in range(2 * n_blk):
            p = jnp.where(sels[pg // 2], jnp.exp(s_ref[pg] - mx), 0.0)
            lacc = lacc + p
            vacc = vacc + p * vall_ref[pg].reshape(hshape)
        den = w_own + jnp.sum(lacc, axis=2, keepdims=True)
        o = (w_own * vnc + jnp.sum(vacc, axis=2, keepdims=True)) / den
        o_ref[0] = jnp.broadcast_to(o, hshape).reshape(D_MODEL, PAGE_SIZE)


def _moba_dec(qt, knt, vnt, cache_k, cache_v, page_table):
    b, n_pages = page_table.shape
    ppb = MOBA_BLOCK // PAGE_SIZE
    n_blk = n_pages // ppb
    pt = page_table.reshape(-1)
    vec = pl.BlockSpec((D_MODEL, b), lambda i, n, pt: (0, 0))

    def page(off):
        return pl.BlockSpec((1, D_MODEL, PAGE_SIZE),
                            lambda i, n, pt: (pt[i * n_pages + n * MOBA_DEC_PAGES + off], 0, 0))

    pages = [page(off) for off in range(MOBA_DEC_PAGES)]
    kern = functools.partial(_moba_dec_kernel, n_blk=n_blk)
    return pl.pallas_call(
        kern,
        grid_spec=pltpu.PrefetchScalarGridSpec(
            num_scalar_prefetch=1,
            grid=(b, n_pages // MOBA_DEC_PAGES),
            in_specs=[vec, vec, vec] + pages + pages,
            out_specs=pl.BlockSpec((1, D_MODEL, PAGE_SIZE), lambda i, n, pt: (i, 0, 0)),
            scratch_shapes=[pltpu.VMEM((D_MODEL, PAGE_SIZE), F32),
                            pltpu.VMEM((n_pages, N_HEADS, 1, PAGE_SIZE), F32),
                            pltpu.VMEM((n_pages, D_MODEL, PAGE_SIZE), F32)]),
        out_shape=jax.ShapeDtypeStruct((b, D_MODEL, PAGE_SIZE), F32),
        compiler_params=_cparams("parallel", "arbitrary"),
        name="moba_dec",
    )(pt, qt, knt, vnt, *([cache_k] * MOBA_DEC_PAGES), *([cache_v] * MOBA_DEC_PAGES))


S5_SLAB = 8
S5_SLAB_W = S5_DIM // S5_SLAB


def _s5_in_kernel(x_ref, g_ref, wre_ref, wim_ref, re_ref, im_ref):
    h = _rms(x_ref[...], g_ref[...]).astype(BF16)
    for j in range(S5_SLAB):
        hj = h[:, j * LANE:(j + 1) * LANE]
        re_ref[:, j, :] = _dot(hj, wre_ref[j])
        im_ref[:, j, :] = _dot(hj, wim_ref[j])


S5_TOKEN = (S5_SLAB, S5_SLAB_W)


def _s5_in(x, g, wre, wim):
    m = x.shape[0]
    tm = _row_tile(m)
    spec = pl.BlockSpec((tm,) + S5_TOKEN, lambda i: (i, 0, 0))
    return pl.pallas_call(
        _s5_in_kernel,
        grid=(m // tm,),
        in_specs=[pl.BlockSpec((tm, D_MODEL), lambda i: (i, 0)), _resident((1, D_MODEL)),
                  _resident((S5_SLAB, LANE, S5_SLAB_W)), _resident((S5_SLAB, LANE, S5_SLAB_W))],
        out_specs=[spec, spec],
        out_shape=[jax.ShapeDtypeStruct((m,) + S5_TOKEN, F32)] * 2,
        compiler_params=_cparams("parallel"),
        name="s5_in",
    )(x, g, wre, wim)


def _s5_scan_kernel(bre_ref, bim_ref, ar_ref, ai_ref, xre_ref, xim_ref, st_ref, *, batch, steps):
    @pl.when(pl.program_id(0) == 0)
    def _():
        st_ref[...] = jnp.zeros_like(st_ref)

    ar = ar_ref[...]
    ai = ai_ref[...]

    def step(t, carry):
        new = []
        for b in range(batch):
            xr, xi = carry[2 * b], carry[2 * b + 1]
            nr = ar * xr - ai * xi + bre_ref[b, t]
            ni = ar * xi + ai * xr + bim_ref[b, t]
            xre_ref[b, t] = nr
            xim_ref[b, t] = ni
            new += [nr, ni]
        return tuple(new)

    init = tuple(st_ref[i] for i in range(2 * batch))
    fin = lax.fori_loop(0, steps, step, init)
    for i in range(2 * batch):
        st_ref[i] = fin[i]


def _s5_scan(bu_re, bu_im, abar_re, abar_im, batch, seq):
    steps = 32
    shape = (batch, seq, S5_SLAB, S5_SLAB_W)
    spec = pl.BlockSpec((batch, steps, S5_SLAB, S5_SLAB_W), lambda t: (0, t, 0, 0))
    kern = functools.partial(_s5_scan_kernel, batch=batch, steps=steps)
    xre, xim = pl.pallas_call(
        kern,
        grid=(seq // steps,),
        in_specs=[spec, spec, _resident((S5_SLAB, S5_SLAB_W)), _resident((S5_SLAB, S5_SLAB_W))],
        out_specs=[spec, spec],
        out_shape=[jax.ShapeDtypeStruct(shape, F32)] * 2,
        scratch_shapes=[pltpu.VMEM((2 * batch, S5_SLAB, S5_SLAB_W), F32)],
        compiler_params=_cparams("arbitrary"),
        name="s5_scan",
    )(bu_re.reshape(shape), bu_im.reshape(shape), abar_re, abar_im)
    return xre.reshape((batch * seq,) + S5_TOKEN), xim.reshape((batch * seq,) + S5_TOKEN)


def _s5_step_kernel(bre_ref, bim_ref, hr_ref, hi_ref, ar_ref, ai_ref, xre_ref, xim_ref):
    ar, ai, hr, hi = ar_ref[...], ai_ref[...], hr_ref[...], hi_ref[...]
    xre_ref[...] = ar * hr - ai * hi + bre_ref[...]
    xim_ref[...] = ar * hi + ai * hr + bim_ref[...]


def _s5_step(bu_re, bu_im, h_re, h_im, abar_re, abar_im):
    m = bu_re.shape[0]
    full = _resident((m,) + S5_TOKEN)
    vec = _resident(S5_TOKEN)
    return pl.pallas_call(
        _s5_step_kernel,
        grid=(1,),
        in_specs=[full, full, full, full, vec, vec],
        out_specs=[pl.BlockSpec((m,) + S5_TOKEN, lambda i: (0, 0, 0))] * 2,
        out_shape=[jax.ShapeDtypeStruct((m,) + S5_TOKEN, F32)] * 2,
        compiler_params=_cparams("arbitrary"),
        name="s5_step",
    )(bu_re, bu_im, h_re, h_im, abar_re, abar_im)


def _s5_out_kernel(x_ref, g_ref, xre_ref, xim_ref, cre_ref, cim_ref, d_ref, wglu_ref, o_ref):
    x = x_ref[...]
    h = _rms(x, g_ref[...])
    ys = []
    for j in range(S5_SLAB):
        cols = slice(j * S5_SLAB_W, (j + 1) * S5_SLAB_W)
        ys.append(_dot(xre_ref[:, cols].astype(BF16), cre_ref[j]) - _dot(xim_ref[:, cols].astype(BF16), cim_ref[j]))
    y = jnp.concatenate(ys, axis=1) + d_ref[...] * h
    gl = _dot(jax.nn.gelu(y).astype(BF16), wglu_ref[...])
    o_ref[...] = x + gl[:, :D_MODEL] * jax.nn.sigmoid(gl[:, D_MODEL:])


def _s5_out(x, g, xre, xim, cre, cim, d, wglu):
    m = x.shape[0]
    tm = _row_tile(m)
    xre, xim = xre.reshape(m, S5_DIM), xim.reshape(m, S5_DIM)
    sspec = pl.BlockSpec((tm, S5_DIM), lambda i: (i, 0))
    return pl.pallas_call(
        _s5_out_kernel,
        grid=(m // tm,),
        in_specs=[pl.BlockSpec((tm, D_MODEL), lambda i: (i, 0)), _resident((1, D_MODEL)), sspec, sspec,
                  _resident((S5_SLAB, S5_SLAB_W, LANE)), _resident((S5_SLAB, S5_SLAB_W, LANE)),
                  _resident((1, D_MODEL)), _resident((D_MODEL, 2 * D_MODEL))],
        out_specs=pl.BlockSpec((tm, D_MODEL), lambda i: (i, 0)),
        out_shape=jax.ShapeDtypeStruct((m, D_MODEL), F32),
        compiler_params=_cparams("parallel"),
        name="s5_out",
    )(x, g, xre, xim, cre, cim, d, wglu)


def _s5_params(a_re, a_im, log_dt, b_re, b_im, c_re, c_im):
    dt = jnp.exp(log_dt)[:, None]
    mag = jnp.exp(a_re * dt)
    abar_re = mag * jnp.cos(a_im * dt)
    abar_im = mag * jnp.sin(a_im * dt)
    den = a_re * a_re + a_im * a_im
    f_re = ((abar_re - 1.0) * a_re + abar_im * a_im) / den
    f_im = (abar_im * a_re - (abar_re - 1.0) * a_im) / den
    bbar_re = f_re[..., None] * b_re - f_im[..., None] * b_im
    bbar_im = f_re[..., None] * b_im + f_im[..., None] * b_re
    eye = jnp.eye(S5_SLAB, dtype=F32)

    def in_w(bb):
        bb = bb.reshape(S5_SLAB, S5_SLAB, S5_STATE, S5_GROUP)
        w = jnp.einsum("jgpc,gh->jgchp", bb, eye)
        return w.reshape(S5_SLAB, LANE, S5_SLAB_W).astype(BF16)

    def out_w(cc):
        cc = cc.reshape(S5_SLAB, S5_SLAB, S5_GROUP, S5_STATE)
        w = jnp.einsum("jgcp,gh->jgphc", cc, eye)
        return w.reshape(S5_SLAB, S5_SLAB_W, LANE).astype(BF16)

    return dict(abar_re=abar_re.reshape(S5_SLAB, S5_SLAB_W), abar_im=abar_im.reshape(S5_SLAB, S5_SLAB_W),
                wre=in_w(bbar_re), wim=in_w(bbar_im), cre=out_w(c_re), cim=out_w(c_im))


DIL_TQ = 128


def _dil_kernel(q_ref, kp_ref, kc_ref, vp_ref, vc_ref, o_ref, lse_ref, *, tq):
    t = pl.program_id(2)
    lane = lax.broadcasted_iota(jnp.int32, (DIL_TQ, LANE), 1)
    row = lax.broadcasted_iota(jnp.int32, (DIL_TQ, 2 * DIL_TQ), 0)
    col = lax.broadcasted_iota(jnp.int32, (DIL_TQ, 2 * DIL_TQ), 1)
    band = jnp.logical_and(col >= row, col <= row + DIL_TAPS)
    first = jnp.logical_and(band, jnp.logical_or(t > 0, col >= DIL_TQ))
    for hp in range(D_MODEL // LANE):
        cols = slice(hp * LANE, (hp + 1) * LANE)
        q_all = q_ref[0, :, cols]
        kcat = jnp.concatenate([kp_ref[0, :, cols], kc_ref[0, :, cols]], axis=0)
        vcat = jnp.concatenate([vp_ref[0, :, cols], vc_ref[0, :, cols]], axis=0)
        for sub in range(tq // DIL_TQ):
            q = q_all[sub * DIL_TQ:(sub + 1) * DIL_TQ]
            keys = kcat[sub * DIL_TQ:(sub + 2) * DIL_TQ]
            vals = vcat[sub * DIL_TQ:(sub + 2) * DIL_TQ]
            ok = first if sub == 0 else band
            o_h, lse_h = [], []
            for hd in range(2):
                mine = (lane < HEAD_DIM) if hd == 0 else (lane >= HEAD_DIM)
                qh = jnp.where(mine, q, jnp.zeros_like(q))
                s = lax.dot_general(qh, keys, _NT, preferred_element_type=F32)
                s = jnp.where(ok, s, -jnp.inf)
                m = jnp.max(s, axis=1, keepdims=True)
                p = jnp.exp(s - m)
                l = jnp.sum(p, axis=1, keepdims=True)
                o_h.append(_dot(p.astype(BF16), vals) / l)
                lse_h.append(m + jnp.log(l))
            rows = slice(sub * DIL_TQ, (sub + 1) * DIL_TQ)
            o_ref[0, rows, cols] = jnp.where(lane < HEAD_DIM, o_h[0], o_h[1]).astype(BF16)
            lse_ref[0, rows, cols] = jnp.where(lane < HEAD_DIM, lse_h[0], lse_h[1])


def _dil_attn(q, k, v, g, dil, batch, seq):
    sd = seq // dil
    tq = min(2 * DIL_TQ, sd)
    qv = q.reshape(batch, sd, dil * D_MODEL)
    kv = k.reshape(batch, sd, dil * D_MODEL)
    vv = v.reshape(batch, sd, dil * D_MODEL)
    sub = tq // DIL_TQ
    cur = pl.BlockSpec((1, tq, D_MODEL), lambda b, r, t: (b, t, r))
    prev = pl.BlockSpec((1, DIL_TQ, D_MODEL), lambda b, r, t: (b, jnp.maximum(t * sub - 1, 0), r))
    ospec = cur
    kern = functools.partial(_dil_kernel, tq=tq)
    o, lse = pl.pallas_call(
        kern,
        grid=(batch, dil, sd // tq),
        in_specs=[cur, prev, cur, prev, cur],
        out_specs=[ospec, ospec],
        out_shape=[jax.ShapeDtypeStruct((batch, sd, dil * D_MODEL), BF16),
                   jax.ShapeDtypeStruct((batch, sd, dil * D_MODEL), F32)],
        compiler_params=_cparams("parallel", "parallel", "parallel"),
        name=f"dil_attn{g}",
    )(qv, kv, kv, vv, vv)
    return o.reshape(batch * seq, D_MODEL), lse.reshape(batch * seq, D_MODEL)


def _dil_out_kernel(o0_ref, o1_ref, o2_ref, l0_ref, l1_ref, l2_ref, x_ref, w_ref, out_ref):
    ls = [l0_ref[...], l1_ref[...], l2_ref[...]]
    mx = jnp.maximum(jnp.maximum(ls[0], ls[1]), ls[2])
    es = [jnp.exp(l - mx) for l in ls]
    den = es[0] + es[1] + es[2]
    os_ = [o0_ref[...], o1_ref[...], o2_ref[...]]
    mix = sum((e / den) * o.astype(F32) for e, o in zip(es, os_))
    out_ref[...] = x_ref[...] + _dot(mix.astype(BF16), w_ref[...])


def _dil_out(os_, ls, x, w):
    m = x.shape[0]
    tm = _row_tile(m)
    spec = pl.BlockSpec((tm, D_MODEL), lambda i: (i, 0))
    return pl.pallas_call(
        _dil_out_kernel,
        grid=(m // tm,),
        in_specs=[spec] * 7 + [_resident((D_MODEL, D_MODEL))],
        out_specs=spec,
        out_shape=jax.ShapeDtypeStruct((m, D_MODEL), F32),
        compiler_params=_cparams("parallel"),
        name="dil_out",
    )(*os_, *ls, x, w)


WIN_HEADS = 4


def _win_dec_kernel(qt_ref, knt_ref, vnt_ref, k0_ref, v0_ref, k1_ref, v1_ref, k2_ref, v2_ref,
                    o_ref, nk0_ref, nv0_ref, nk1_ref, nv1_ref, nk2_ref, nv2_ref):
    b, hc = pl.program_id(0), pl.program_id(1)
    rows = WIN_HEADS * HEAD_DIM
    cshape = (WIN_HEADS, HEAD_DIM, 1)
    caches = ((k0_ref, v0_ref, nk0_ref, nv0_ref), (k1_ref, v1_ref, nk1_ref, nv1_ref),
              (k2_ref, v2_ref, nk2_ref, nv2_ref))
    cols, ss, sn = [], [], []
    for g, (kc, _, _, _) in enumerate(caches):
        win, dil = DIL_PAIRS[g]
        sl = pl.ds(pl.multiple_of(g * D_MODEL + hc * rows, rows), rows)
        qc = _batch_column(qt_ref, b, sl).reshape(cshape)
        knc = _batch_column(knt_ref, b, sl).reshape(cshape)
        vnc = _batch_column(vnt_ref, b, sl).reshape(cshape)
        cols.append((knc, vnc))
        s = jnp.sum(kc[0].reshape(WIN_HEADS, HEAD_DIM, win) * qc, axis=1, keepdims=True)
        r = lax.broadcasted_iota(jnp.int32, s.shape, 2)
        ss.append(jnp.where((r & (dil - 1)) == 0, s, -jnp.inf))
        sn.append(jnp.sum(qc * knc, axis=1, keepdims=True))
    mx = sn[0]
    for g in range(N_DIL):
        mx = jnp.maximum(mx, jnp.maximum(sn[g], jnp.max(ss[g], axis=2, keepdims=True)))
    den = jnp.zeros((WIN_HEADS, 1, 1), F32)
    num = jnp.zeros(cshape, F32)
    for g, (kc, vc, nk, nv) in enumerate(caches):
        win, _ = DIL_PAIRS[g]
        knc, vnc = cols[g]
        p = jnp.exp(ss[g] - mx)
        pn = jnp.exp(sn[g] - mx)
        den = den + jnp.sum(p, axis=2, keepdims=True) + pn
        num = num + jnp.sum(p * vc[0].reshape(WIN_HEADS, HEAD_DIM, win), axis=2, keepdims=True) + pn * vnc
        last = lax.broadcasted_iota(jnp.int32, (rows, win), 1) == win - 1
        nk[0] = jnp.where(last, knc.reshape(rows, 1), pltpu.roll(kc[0], win - 1, 1))
        nv[0] = jnp.where(last, vnc.reshape(rows, 1), pltpu.roll(vc[0], win - 1, 1))
    o_ref[0] = jnp.broadcast_to(num / den, (WIN_HEADS, HEAD_DIM, LANE)).reshape(rows, LANE)


def _win_dec(qt, knt, vnt, caches):
    b = qt.shape[1]
    rows = WIN_HEADS * HEAD_DIM
    vec = pl.BlockSpec((N_DIL * D_MODEL, b), lambda i, hc: (0, 0))
    cspecs, cshapes = [], []
    for win, _ in DIL_PAIRS:
        cspecs += [pl.BlockSpec((1, rows, win), lambda i, hc: (i, hc, 0))] * 2
        cshapes += [jax.ShapeDtypeStruct((b, D_MODEL, win), F32)] * 2
    outs = pl.pallas_call(
        _win_dec_kernel,
        grid=(b, D_MODEL // rows),
        in_specs=[vec, vec, vec] + cspecs,
        out_specs=[pl.BlockSpec((1, rows, LANE), lambda i, hc: (i, hc, 0))] + cspecs,
        out_shape=[jax.ShapeDtypeStruct((b, D_MODEL, LANE), F32)] + cshapes,
        compiler_params=_cparams("parallel", "parallel"),
        name="win_dec",
    )(qt, knt, vnt, *caches)
    return outs[0], outs[1:]


def _prep_weights(W):
    P = {}
    row = lambda a: a.reshape(1, -1)
    for i in range(4):
        wup = W["f_w_up"][i].astype(BF16)
        wcv = jnp.pad(W["f_w_conv"][i], ((0, SUBLANE - 3), (0, 0)))
        wdn = W["f_w_down"][i].astype(BF16)
        P[f"ffn{i}"] = (row(W["norm_ffn"][i]), wup, wcv, wdn, row(W["norm_ple"][i]),
                        W["ple_w_gate"][i].astype(BF16), W["ple_w_proj"][i].astype(BF16), row(W["norm_final"]))
    P["norm_mix"] = [row(W["norm_mix"][i]) for i in range(4)]
    P["a_w_in"] = W["a_w_in"][0].astype(BF16)
    P["a_b_in"] = row(W["a_b_in"][0])
    P["a_w_dw"] = jnp.pad(W["a_w_dw"][0], ((0, CONV_HALO - CONV_A_WIDTH), (0, 0)))
    P["a_b_dw"] = row(W["a_b_dw"][0])
    P["a_ln_g"] = row(W["a_ln_g"][0])
    P["a_ln_b"] = row(W["a_ln_b"][0])
    P["a_w_out"] = W["a_w_out"][0].astype(BF16)
    P["b_w_qkv"] = W["b_w_qkv"][0].astype(BF16)
    P["b_w_o"] = W["b_w_o"][0].astype(BF16)
    P["s5"] = _s5_params(W["c_a_re"][0], W["c_a_im"][0], W["c_log_dt"][0], W["c_b_re"][0], W["c_b_im"][0],
                         W["c_c_re"][0], W["c_c_im"][0])
    P["c_d"] = row(W["c_d"][0])
    P["c_w_glu"] = W["c_w_glu"][0].astype(BF16)
    P["d_w_qkv"] = W["d_w_qkv"][0].astype(BF16)
    P["d_w_o"] = W["d_w_o"][0].astype(BF16)
    return P


def _heads(a, batch, rows):
    return a.reshape(1, batch, rows, N_HEADS, HEAD_DIM)


def _heads_t(a, batch, rows):
    return a.reshape(batch, N_HEADS, HEAD_DIM, rows).transpose(0, 3, 1, 2)[None]


def _cache_t(c):
    n, rows = c.shape[0], c.shape[1]
    return c.transpose(0, 2, 3, 1).reshape(n, D_MODEL, rows)


def _prompt_trunk(x, p, P):
    batch, seq, _ = x.shape
    m = batch * seq
    x = x.reshape(m, D_MODEL)
    tabs = _rope_tables(jnp.arange(seq, dtype=jnp.int32))
    st = {}
    u = _conv_in(x, P["norm_mix"][0], P["a_w_in"], P["a_b_in"])
    st["conv_a"] = u.reshape(batch, seq, D_MODEL)[None, :, seq - (CONV_A_WIDTH - 1):]
    x = _conv_out(u, x, seq, P["a_w_dw"], P["a_b_dw"], P["a_ln_g"], P["a_ln_b"], P["a_w_out"])
    ffn_states = []
    x, fs = _ffn_prompt(x, p[0].reshape(m, PLE_DIM), P["ffn0"], batch, seq, False)
    ffn_states.append(fs)
    q, k, v, kt, vt = _qkv(x, P["norm_mix"][1], P["b_w_qkv"], tabs, 1, seq, batch)
    st["moba_k"] = _heads_t(kt, batch, seq)
    st["moba_v"] = _heads_t(vt, batch, seq)
    o = _moba_attn(q[0].reshape(batch, seq, D_MODEL), k[0].reshape(batch, seq, D_MODEL),
                   v[0].reshape(batch, seq, D_MODEL), batch, seq)
    x = _proj(o.reshape(m, D_MODEL), x, P["b_w_o"])
    x, fs = _ffn_prompt(x, p[1].reshape(m, PLE_DIM), P["ffn1"], batch, seq, False)
    ffn_states.append(fs)
    s5 = P["s5"]
    bu_re, bu_im = _s5_in(x, P["norm_mix"][2], s5["wre"], s5["wim"])
    xre, xim = _s5_scan(bu_re, bu_im, s5["abar_re"], s5["abar_im"], batch, seq)
    last = lambda a: a.reshape((batch, seq) + S5_TOKEN)[:, seq - 1].reshape(1, batch, S5_GROUPS, S5_STATE)
    st["s5_re"] = last(xre)
    st["s5_im"] = last(xim)
    x = _s5_out(x, P["norm_mix"][2], xre, xim, s5["cre"], s5["cim"], P["c_d"], P["c_w_glu"])
    x, fs = _ffn_prompt(x, p[2].reshape(m, PLE_DIM), P["ffn2"], batch, seq, False)
    ffn_states.append(fs)
    q, k, v, kt, vt = _qkv(x, P["norm_mix"][3], P["d_w_qkv"], tabs, N_DIL, seq, batch)
    os_, ls = [], []
    for g, (win, dil) in enumerate(DIL_PAIRS):
        keep = min(win, seq)
        feat = slice(g * D_MODEL, (g + 1) * D_MODEL)
        st[f"win_k{g + 1}"] = _heads_t(kt[:, feat, seq - keep:], batch, keep)
        st[f"win_v{g + 1}"] = _heads_t(vt[:, feat, seq - keep:], batch, keep)
        o, lse = _dil_attn(q[g], k[g], v[g], g, dil, batch, seq)
        os_.append(o)
        ls.append(lse)
    x = _dil_out(os_, ls, x, P["d_w_o"])
    x, fs = _ffn_prompt(x, p[3].reshape(m, PLE_DIM), P["ffn3"], batch, seq, True)
    ffn_states.append(fs)
    st["ffn_conv"] = jnp.stack(ffn_states, 0)
    return x.reshape(batch, seq, D_MODEL), st


def _sample_trunk(x, p, P, S, past_len):
    b = x.shape[0]
    x = x.reshape(b, D_MODEL)
    tabs = _rope_tables(jnp.full((b,), past_len, dtype=jnp.int32))
    st = {}
    buf = S["conv_a"][0]
    x, u = _conv_dec(x, P["norm_mix"][0], P["a_w_in"], P["a_b_in"], buf.transpose(1, 0, 2), P["a_w_dw"],
                     P["a_b_dw"], P["a_ln_g"], P["a_ln_b"], P["a_w_out"])
    st["conv_a"] = jnp.concatenate([buf[:, 1:], u[:, None]], axis=1)[None]
    ffn_states = []
    x, fs = _ffn_dec(x, p[0].reshape(b, PLE_DIM), P["ffn0"], S["ffn_conv"][0], False)
    ffn_states.append(fs)
    q, k, v = _qkv(x, P["norm_mix"][1], P["b_w_qkv"], tabs, 1, b)
    st["moba_k"] = _heads(k, b, 1)
    st["moba_v"] = _heads(v, b, 1)
    o = _moba_dec(q.astype(F32).T, k.T, v.T, _cache_t(S["moba_k"][0]), _cache_t(S["moba_v"][0]),
                  S["page_table"])
    x = _proj(o[:, :, 0].astype(BF16), x, P["b_w_o"])
    x, fs = _ffn_dec(x, p[1].reshape(b, PLE_DIM), P["ffn1"], S["ffn_conv"][1], False)
    ffn_states.append(fs)
    s5 = P["s5"]
    bu_re, bu_im = _s5_in(x, P["norm_mix"][2], s5["wre"], s5["wim"])
    xre, xim = _s5_step(bu_re, bu_im, S["s5_re"][0].reshape((b,) + S5_TOKEN),
                        S["s5_im"][0].reshape((b,) + S5_TOKEN), s5["abar_re"], s5["abar_im"])
    st["s5_re"] = xre.reshape(1, b, S5_GROUPS, S5_STATE)
    st["s5_im"] = xim.reshape(1, b, S5_GROUPS, S5_STATE)
    x = _s5_out(x, P["norm_mix"][2], xre, xim, s5["cre"], s5["cim"], P["c_d"], P["c_w_glu"])
    x, fs = _ffn_dec(x, p[2].reshape(b, PLE_DIM), P["ffn2"], S["ffn_conv"][2], False)
    ffn_states.append(fs)
    q, k, v = _qkv(x, P["norm_mix"][3], P["d_w_qkv"], tabs, N_DIL, b)
    caches = []
    for g in range(N_DIL):
        caches += [_cache_t(S[f"win_k{g + 1}"][0]), _cache_t(S[f"win_v{g + 1}"][0])]
    o, new = _win_dec(q.astype(F32).T, k.T, v.T, caches)
    for g, (win, _) in enumerate(DIL_PAIRS):
        st[f"win_k{g + 1}"] = _heads_t(new[2 * g], b, win)
        st[f"win_v{g + 1}"] = _heads_t(new[2 * g + 1], b, win)
    x = _proj(o[:, :, 0].astype(BF16), x, P["d_w_o"])
    x, fs = _ffn_dec(x, p[3].reshape(b, PLE_DIM), P["ffn3"], S["ffn_conv"][3], True)
    ffn_states.append(fs)
    st["ffn_conv"] = jnp.stack(ffn_states, 0)
    return x.reshape(b, 1, D_MODEL), st


_STATE_KEYS = ("conv_a", "moba_k", "moba_v", "s5_re", "s5_im", "win_k1", "win_v1", "win_k2", "win_v2",
               "win_k3", "win_v3", "ffn_conv")


def kernel(x_prompt, x_sample, state_conv_a, cache_moba_k, cache_moba_v, state_s5_re, state_s5_im, cache_win_k1, cache_win_v1, cache_win_k2, cache_win_v2, cache_win_k3, cache_win_v3, state_ffn_conv, page_table, p_prompt, p_sample, norm_mix, norm_ffn, norm_ple, norm_final, a_w_in, a_b_in, a_w_dw, a_b_dw, a_ln_g, a_ln_b, a_w_out, b_w_qkv, b_w_o, c_a_re, c_a_im, c_log_dt, c_b_re, c_b_im, c_c_re, c_c_im, c_d, c_w_glu, d_w_qkv, d_w_o, f_w_up, f_w_conv, f_w_down, ple_w_proj, ple_w_gate):
    W = dict(norm_mix=norm_mix, norm_ffn=norm_ffn, norm_ple=norm_ple, norm_final=norm_final,
             a_w_in=a_w_in, a_b_in=a_b_in, a_w_dw=a_w_dw, a_b_dw=a_b_dw, a_ln_g=a_ln_g, a_ln_b=a_ln_b,
             a_w_out=a_w_out, b_w_qkv=b_w_qkv, b_w_o=b_w_o,
             c_a_re=c_a_re, c_a_im=c_a_im, c_log_dt=c_log_dt, c_b_re=c_b_re, c_b_im=c_b_im,
             c_c_re=c_c_re, c_c_im=c_c_im, c_d=c_d, c_w_glu=c_w_glu,
             d_w_qkv=d_w_qkv, d_w_o=d_w_o, f_w_up=f_w_up, f_w_conv=f_w_conv, f_w_down=f_w_down,
             ple_w_proj=ple_w_proj, ple_w_gate=ple_w_gate)
    S = dict(conv_a=state_conv_a, moba_k=cache_moba_k, moba_v=cache_moba_v, s5_re=state_s5_re, s5_im=state_s5_im,
             win_k1=cache_win_k1, win_v1=cache_win_v1, win_k2=cache_win_k2, win_v2=cache_win_v2,
             win_k3=cache_win_k3, win_v3=cache_win_v3, ffn_conv=state_ffn_conv, page_table=page_table)
    P = _prep_weights(W)
    past_len = page_table.shape[1] * PAGE_SIZE
    y_prompt, sp = _prompt_trunk(x_prompt, p_prompt, P)
    y_sample, ss = _sample_trunk(x_sample, p_sample, P, S, past_len)
    return (y_prompt, y_sample) + tuple(sp[n] for n in _STATE_KEYS) + tuple(ss[n] for n in _STATE_KEYS)
```

```python
import functools
import math

import jax
import jax.numpy as jnp
from jax import lax
from jax.experimental import pallas as pl
from jax.experimental.pallas import tpu as pltpu

F32 = jnp.float32
BF16 = jnp.bfloat16

D_MODEL = 1024
N_HEADS = 16
HEAD_DIM = 64
ROT_DIM = 16
ROPE_THETA = 500000.0
EPS = 1e-6
PLE_DIM = 256
CONV_A_WIDTH = 31
MOBA_BLOCK = 256
MOBA_TOPK = 3
S5_GROUP = 16
S5_GROUPS = 64
S5_STATE = 64
S5_DIM = S5_GROUPS * S5_STATE
DIL_PAIRS = ((128, 1), (512, 4), (2048, 16))
N_DIL = 3
DIL_TAPS = 128
D_FF = 2816
PAGE_SIZE = 128

LANE = 128
SUBLANE = 8
FF_CHUNK = 256
N_FF_CHUNK = D_FF // FF_CHUNK
VMEM_LIMIT = 56 * 1024 * 1024
MASK_NEG = -30000.0

_NT = (((1,), (1,)), ((), ()))


def _cparams(*sem):
    return pltpu.CompilerParams(dimension_semantics=sem, vmem_limit_bytes=VMEM_LIMIT)


def _resident(shape):
    nd = len(shape)
    return pl.BlockSpec(shape, lambda *_: (0,) * nd, pipeline_mode=pl.Buffered(1))


def _rms(x, g):
    return x * lax.rsqrt(jnp.mean(x * x, axis=-1, keepdims=True) + EPS) * g


def _dot(a, b):
    return jnp.dot(a, b, preferred_element_type=F32)


def _row_tile(m):
    return 512 if m % 512 == 0 else m


def _conv_in_kernel(x_ref, g_ref, w_ref, b_ref, u_ref):
    h = _rms(x_ref[...], g_ref[...]).astype(BF16)
    z = _dot(h, w_ref[...]) + b_ref[...]
    u_ref[...] = z[:, :D_MODEL] * jax.nn.sigmoid(z[:, D_MODEL:])


def _conv_in(x, g, w, b):
    m = x.shape[0]
    tm = _row_tile(m)
    return pl.pallas_call(
        _conv_in_kernel,
        grid=(m // tm,),
        in_specs=[pl.BlockSpec((tm, D_MODEL), lambda i: (i, 0)),
                  _resident((1, D_MODEL)), _resident((D_MODEL, 2 * D_MODEL)), _resident((1, 2 * D_MODEL))],
        out_specs=pl.BlockSpec((tm, D_MODEL), lambda i: (i, 0)),
        out_shape=jax.ShapeDtypeStruct((m, D_MODEL), F32),
        compiler_params=_cparams("parallel"),
        name="conv_in",
    )(x, g, w, b)


def _ln_silu_out(cf, x, lng, lnb, wout):
    mu = jnp.mean(cf, axis=-1, keepdims=True)
    var = jnp.mean(jnp.square(cf - mu), axis=-1, keepdims=True)
    cn = (cf - mu) * lax.rsqrt(var + EPS) * lng + lnb
    return x + _dot(jax.nn.silu(cn).astype(BF16), wout)


CONV_HALO = 32


def _conv_out_kernel(u_ref, halo_ref, x_ref, wdw_ref, bdw_ref, lng_ref, lnb_ref, wout_ref, o_ref,
                     ext_ref, cf_ref, *, tm, tiles_per_seq):
    first = (pl.program_id(0) % tiles_per_seq) == 0
    ext_ref[0:CONV_HALO, :] = jnp.where(first, 0.0, halo_ref[...])
    ext_ref[CONV_HALO:, :] = u_ref[...]
    off = CONV_HALO - (CONV_A_WIDTH - 1)
    for c in range(D_MODEL // LANE):
        cols = slice(c * LANE, (c + 1) * LANE)
        acc = jnp.zeros((tm, LANE), F32)
        for j in range(CONV_A_WIDTH):
            acc = acc + wdw_ref[j:j + 1, cols] * ext_ref[pl.ds(off + j, tm), cols]
        cf_ref[:, cols] = acc + bdw_ref[:, cols]
    o_ref[...] = _ln_silu_out(cf_ref[...], x_ref[...], lng_ref[...], lnb_ref[...], wout_ref[...])


def _conv_out(u, x, seq, wdw, bdw, lng, lnb, wout):
    m = x.shape[0]
    tm = 256
    tps = seq // tm
    hb = tm // CONV_HALO
    kern = functools.partial(_conv_out_kernel, tm=tm, tiles_per_seq=tps)
    return pl.pallas_call(
        kern,
        grid=(m // tm,),
        in_specs=[pl.BlockSpec((tm, D_MODEL), lambda i: (i, 0)),
                  pl.BlockSpec((CONV_HALO, D_MODEL), lambda i: (jnp.maximum(i * hb - 1, 0), 0)),
                  pl.BlockSpec((tm, D_MODEL), lambda i: (i, 0)),
                  _resident((CONV_HALO, D_MODEL)), _resident((1, D_MODEL)), _resident((1, D_MODEL)),
                  _resident((1, D_MODEL)), _resident((D_MODEL, D_MODEL))],
        out_specs=pl.BlockSpec((tm, D_MODEL), lambda i: (i, 0)),
        out_shape=jax.ShapeDtypeStruct((m, D_MODEL), F32),
        scratch_shapes=[pltpu.VMEM((tm + CONV_HALO, D_MODEL), F32), pltpu.VMEM((tm, D_MODEL), F32)],
        compiler_params=_cparams("parallel"),
        name="conv_out",
    )(u, u, x, wdw, bdw, lng, lnb, wout)


def _conv_dec_kernel(x_ref, g_ref, win_ref, bin_ref, buf_ref, wdw_ref, bdw_ref, lng_ref, lnb_ref, wout_ref,
                     o_ref, u_ref, acc_ref):
    j = pl.program_id(0)

    @pl.when(j == 0)
    def _():
        h = _rms(x_ref[...], g_ref[...]).astype(BF16)
        z = _dot(h, win_ref[...]) + bin_ref[...]
        u = z[:, :D_MODEL] * jax.nn.sigmoid(z[:, D_MODEL:])
        u_ref[...] = u
        acc_ref[...] = wdw_ref[CONV_A_WIDTH - 1:CONV_A_WIDTH, :] * u

    acc_ref[...] += wdw_ref[pl.ds(j, 1), :] * buf_ref[0]

    @pl.when(j == CONV_A_WIDTH - 2)
    def _():
        cf = acc_ref[...] + bdw_ref[...]
        o_ref[...] = _ln_silu_out(cf, x_ref[...], lng_ref[...], lnb_ref[...], wout_ref[...])


def _conv_dec(x, g, win, bin_, buf_t, wdw, bdw, lng, lnb, wout):
    m = x.shape[0]
    return pl.pallas_call(
        _conv_dec_kernel,
        grid=(CONV_A_WIDTH - 1,),
        in_specs=[_resident((m, D_MODEL)), _resident((1, D_MODEL)), _resident((D_MODEL, 2 * D_MODEL)),
                  _resident((1, 2 * D_MODEL)),
                  pl.BlockSpec((1, m, D_MODEL), lambda j: (j, 0, 0)),
                  _resident((CONV_HALO, D_MODEL)), _resident((1, D_MODEL)), _resident((1, D_MODEL)),
                  _resident((1, D_MODEL)), _resident((D_MODEL, D_MODEL))],
        out_specs=[pl.BlockSpec((m, D_MODEL), lambda j: (0, 0)), pl.BlockSpec((m, D_MODEL), lambda j: (0, 0))],
        out_shape=[jax.ShapeDtypeStruct((m, D_MODEL), F32), jax.ShapeDtypeStruct((m, D_MODEL), F32)],
        scratch_shapes=[pltpu.VMEM((m, D_MODEL), F32)],
        compiler_params=_cparams("arbitrary"),
        name="conv_dec",
    )(x, g, win, bin_, buf_t, wdw, bdw, lng, lnb, wout)


def _ffn_tail(x2, p_ref, gple_ref, wgate_ref, wproj_ref, gfin_ref, final):
    hn = _rms(x2, gple_ref[...]).astype(BF16)
    gate = jax.nn.sigmoid(_dot(hn, wgate_ref[...]))
    x3 = x2 + _dot(p_ref[...].astype(BF16), wproj_ref[...]) * gate
    if final:
        x3 = _rms(x3, gfin_ref[...])
    return x3


def _ffn_prompt_kernel(x_ref, p_ref, gffn_ref, wup_ref, wcv_ref, wdn_ref, gple_ref, wgate_ref, wproj_ref,
                       gfin_ref, o_ref, st_ref, h_ref, act_ref, ua_ref, ub_ref, carry_ref,
                       *, tm, tiles_per_seq, final):
    t = pl.program_id(0) % tiles_per_seq

    @pl.when(t == 0)
    def _():
        carry_ref[...] = jnp.zeros_like(carry_ref)

    h_ref[...] = _rms(x_ref[...], gffn_ref[...]).astype(BF16)

    def conv_half(c, ubuf):
        u = _dot(h_ref[...], wup_ref[:, _ff_cols(c)])
        ubuf[0:SUBLANE, :] = carry_ref[c]
        ubuf[SUBLANE:, :] = u
        w = wcv_ref[:, _ff_cols(c)]
        cc = (w[0:1] * ubuf[pl.ds(SUBLANE - 2, tm), :] + w[1:2] * ubuf[pl.ds(SUBLANE - 1, tm), :]
              + w[2:3] * u)
        carry_ref[c] = ubuf[pl.ds(tm, SUBLANE), :]
        return cc

    def chunk(j, slot):
        ca = conv_half(j, ua_ref.at[slot])
        cb = conv_half(j + N_FF_CHUNK, ub_ref.at[slot])
        act_ref[j] = (jax.nn.silu(ca) * cb).astype(BF16)

    for j in range(N_FF_CHUNK):
        chunk(j, j % 2)

    @pl.when(t == tiles_per_seq - 1)
    def _():
        st_ref[0] = carry_ref[...]

    y = _dot(act_ref[0], wdn_ref[_ff_rows(0), :])
    for j in range(1, N_FF_CHUNK):
        y = y + _dot(act_ref[j], wdn_ref[_ff_rows(j), :])
    o_ref[...] = _ffn_tail(x_ref[...] + y, p_ref, gple_ref, wgate_ref, wproj_ref, gfin_ref, final)


def _ffn_dec_kernel(x_ref, p_ref, gffn_ref, wup_ref, wcv_ref, wdn_ref, gple_ref, wgate_ref, wproj_ref,
                    gfin_ref, s0_ref, s1_ref, o_ref, u_ref, h_ref, *, final):
    h_ref[...] = _rms(x_ref[...], gffn_ref[...]).astype(BF16)

    def conv_half(c):
        u = _dot(h_ref[...], wup_ref[:, _ff_cols(c)])
        u_ref[c] = u
        w = wcv_ref[:, _ff_cols(c)]
        return w[0:1] * s0_ref[c] + w[1:2] * s1_ref[c] + w[2:3] * u

    y = None
    for j in range(N_FF_CHUNK):
        act = (jax.nn.silu(conv_half(j)) * conv_half(j + N_FF_CHUNK)).astype(BF16)
        d = _dot(act, wdn_ref[_ff_rows(j), :])
        y = d if y is None else y + d
    o_ref[...] = _ffn_tail(x_ref[...] + y, p_ref, gple_ref, wgate_ref, wproj_ref, gfin_ref, final)


def _ff_cols(c):
    return slice(c * FF_CHUNK, (c + 1) * FF_CHUNK)


_ff_rows = _ff_cols


def _ffn_weight_specs():
    return [_resident((1, D_MODEL)),
            _resident((D_MODEL, 2 * D_FF)),
            _resident((SUBLANE, 2 * D_FF)),
            _resident((D_FF, D_MODEL)),
            _resident((1, D_MODEL)), _resident((D_MODEL, D_MODEL)), _resident((PLE_DIM, D_MODEL)),
            _resident((1, D_MODEL))]


def _ffn_prompt(x, p, wts, batch, seq, final):
    m = x.shape[0]
    tm = _row_tile(seq)
    tps = seq // tm
    kern = functools.partial(_ffn_prompt_kernel, tm=tm, tiles_per_seq=tps, final=final)
    nc = 2 * N_FF_CHUNK
    out, st = pl.pallas_call(
        kern,
        grid=(m // tm,),
        in_specs=[pl.BlockSpec((tm, D_MODEL), lambda i: (i, 0)),
                  pl.BlockSpec((tm, PLE_DIM), lambda i: (i, 0))] + _ffn_weight_specs(),
        out_specs=[pl.BlockSpec((tm, D_MODEL), lambda i: (i, 0)),
                   pl.BlockSpec((1, nc, SUBLANE, FF_CHUNK), lambda i: (i // tps, 0, 0, 0))],
        out_shape=[jax.ShapeDtypeStruct((m, D_MODEL), F32),
                   jax.ShapeDtypeStruct((batch, nc, SUBLANE, FF_CHUNK), F32)],
        scratch_shapes=[pltpu.VMEM((tm, D_MODEL), BF16), pltpu.VMEM((N_FF_CHUNK, tm, FF_CHUNK), BF16),
                        pltpu.VMEM((2, tm + SUBLANE, FF_CHUNK), F32), pltpu.VMEM((2, tm + SUBLANE, FF_CHUNK), F32),
                        pltpu.VMEM((nc, SUBLANE, FF_CHUNK), F32)],
        compiler_params=_cparams("arbitrary"),
        name="ffn_prompt",
    )(x, p, *wts)
    new_state = st[:, :, SUBLANE - 2:, :].transpose(0, 2, 1, 3).reshape(batch, 2, 2 * D_FF)
    return out, new_state


def _ffn_dec(x, p, wts, state, final):
    m = x.shape[0]
    nc = 2 * N_FF_CHUNK
    kern = functools.partial(_ffn_dec_kernel, final=final)
    s = state.reshape(m, 2, nc, FF_CHUNK).transpose(1, 2, 0, 3)
    out, u = pl.pallas_call(
        kern,
        grid=(1,),
        in_specs=[_resident((m, D_MODEL)), _resident((m, PLE_DIM))] + _ffn_weight_specs()
        + [_resident((nc, m, FF_CHUNK)), _resident((nc, m, FF_CHUNK))],
        out_specs=[pl.BlockSpec((m, D_MODEL), lambda i: (0, 0)),
                   pl.BlockSpec((nc, m, FF_CHUNK), lambda i: (0, 0, 0))],
        out_shape=[jax.ShapeDtypeStruct((m, D_MODEL), F32), jax.ShapeDtypeStruct((nc, m, FF_CHUNK), F32)],
        scratch_shapes=[pltpu.VMEM((m, D_MODEL), BF16)],
        compiler_params=_cparams("arbitrary"),
        name="ffn_dec",
    )(x, p, *wts, s[0], s[1])
    u_rows = u.transpose(1, 0, 2).reshape(m, 1, 2 * D_FF)
    return out, jnp.concatenate([state[:, 1:], u_rows], axis=1)


def _qkv_kernel(x_ref, g_ref, w_ref, c_ref, s1_ref, s2_ref, q_ref, k_ref, v_ref, *t_refs, dil=1):
    h = _rms(x_ref[...], g_ref[...]).astype(BF16)
    reps = D_MODEL // LANE
    cos = jnp.concatenate([c_ref[...]] * reps, axis=1)
    sn1 = jnp.concatenate([s1_ref[...]] * reps, axis=1)
    sn2 = jnp.concatenate([s2_ref[...]] * reps, axis=1)
    half = ROT_DIM // 2

    def rope(z):
        return (z * cos + pltpu.roll(z, D_MODEL - half, 1) * sn1 + pltpu.roll(z, half, 1) * sn2)

    def put(ref, val):
        if dil == 1:
            ref[...] = val.astype(ref.dtype)
            return
        stage = t_refs[2]
        n = val.shape[0] // dil
        for c in range(reps):
            stage[c] = val[:, c * LANE:(c + 1) * LANE]
        for r in range(dil):
            for c in range(reps):
                lo = r * D_MODEL + c * LANE
                ref[:, lo:lo + LANE] = stage[c, pl.ds(r, n, stride=dil), :].astype(ref.dtype)

    q = rope(_dot(h, w_ref[:, 0:D_MODEL]))
    put(q_ref, q * (HEAD_DIM ** -0.5))
    k = rope(_dot(h, w_ref[:, D_MODEL:2 * D_MODEL]))
    v = _dot(h, w_ref[:, 2 * D_MODEL:])
    put(k_ref, k)
    put(v_ref, v)
    if t_refs:
        kt_ref, vt_ref = t_refs[:2]
        kt_ref[0] = k.T
        vt_ref[0] = v.T


def _qkv(x, g, w, tabs, n_groups, seq, batch=None, group0=0, dil=1):
    m = x.shape[0]
    tm = _row_tile(seq)
    npb = seq // tm
    row = lambda gi, i: (i, 0)
    col = lambda gi, i: (i, gi)
    tab = pl.BlockSpec((tm, LANE), lambda gi, i: (i % npb, 0))
    n = n_groups * D_MODEL
    scratch = []
    if batch is None:
        out_specs = [pl.BlockSpec((tm, D_MODEL), col)] * 3
        out_shape = [jax.ShapeDtypeStruct((m, n), BF16)] + [jax.ShapeDtypeStruct((m, n), F32)] * 2
    else:
        if dil == 1:
            out_specs = [pl.BlockSpec((None, tm, D_MODEL), lambda gi, i: (gi, i, 0))] * 3
            out_shape = [jax.ShapeDtypeStruct((n_groups, m, D_MODEL), BF16)] * 3
        else:
            assert n_groups == 1
            out_specs = [pl.BlockSpec((None, tm // dil, dil * D_MODEL), lambda gi, i: (i // npb, i % npb, 0))] * 3
            out_shape = [jax.ShapeDtypeStruct((batch, seq // dil, dil * D_MODEL), BF16)] * 3
            scratch = [pltpu.VMEM((D_MODEL // LANE, tm, LANE), F32)]
        out_specs += [pl.BlockSpec((1, D_MODEL, tm), lambda gi, i: (i // npb, gi, i % npb))] * 2
        out_shape += [jax.ShapeDtypeStruct((batch, n, seq), F32)] * 2
    return pl.pallas_call(
        functools.partial(_qkv_kernel, dil=dil),
        grid=(n_groups, m // tm),
        in_specs=[pl.BlockSpec((tm, D_MODEL), row), pl.BlockSpec((1, D_MODEL), lambda gi, i: (0, 0)),
                  pl.BlockSpec((D_MODEL, 3 * D_MODEL), lambda gi, i: (0, group0 + gi)), tab, tab, tab],
        out_specs=out_specs,
        out_shape=out_shape,
        scratch_shapes=scratch,
        compiler_params=_cparams("parallel", "parallel"),
        name="qkv_rope",
    )(x, g, w, *tabs)


def _rope_tables(pos):
    half = ROT_DIM // 2
    inv_freq = ROPE_THETA ** (-jnp.arange(half, dtype=F32) / half)
    ang = pos.astype(F32)[:, None] * inv_freq[None, :]
    cos, sin = jnp.cos(ang), jnp.sin(ang)
    lane = jnp.arange(LANE) % HEAD_DIM
    idx = lane % half
    c = jnp.where(lane[None, :] < ROT_DIM, cos[:, idx], 1.0)
    s1 = jnp.where(lane[None, :] < half, -sin[:, idx], 0.0)
    s2 = jnp.where((lane[None, :] >= half) & (lane[None, :] < ROT_DIM), sin[:, idx], 0.0)
    return c, s1, s2


def _proj_kernel(a_ref, x_ref, w_ref, o_ref):
    o_ref[...] = x_ref[...] + _dot(a_ref[...], w_ref[...])


def _proj(a, x, w):
    m = x.shape[0]
    tm = _row_tile(m)
    return pl.pallas_call(
        _proj_kernel,
        grid=(m // tm,),
        in_specs=[pl.BlockSpec((tm, D_MODEL), lambda i: (i, 0)), pl.BlockSpec((tm, D_MODEL), lambda i: (i, 0)),
                  _resident((D_MODEL, D_MODEL))],
        out_specs=pl.BlockSpec((tm, D_MODEL), lambda i: (i, 0)),
        out_shape=jax.ShapeDtypeStruct((m, D_MODEL), F32),
        compiler_params=_cparams("parallel"),
        name="out_proj",
    )(a, x, w)


def _moba_kernel(q_ref, k_ref, v_ref, o_ref, ka0_ref, ka1_ref, va0_ref, va1_ref, km_ref, *, seq):
    nb = seq // MOBA_BLOCK
    qi = pl.program_id(2)
    blk = MOBA_BLOCK
    span = 2 * blk

    @pl.when(qi == 0)
    def _():
        kf = k_ref[0].astype(F32)
        rblk = lax.broadcasted_iota(jnp.int32, (seq, LANE), 0) >> int(math.log2(blk))
        lane = lax.broadcasted_iota(jnp.int32, (seq, LANE), 1)
        ka0_ref[...] = jnp.where(lane < HEAD_DIM, kf, (lane - HEAD_DIM == rblk).astype(F32)).astype(BF16)
        ka1_ref[...] = jnp.where(lane >= HEAD_DIM, kf, (lane == rblk).astype(F32)).astype(BF16)
        km_ref[...] = jnp.mean(kf.reshape(nb, blk, LANE), axis=1)
        vf = v_ref[0].astype(F32)
        va0_ref[...] = jnp.where(lane < HEAD_DIM, vf, 1.0).astype(BF16)
        va1_ref[...] = jnp.where(lane >= HEAD_DIM, vf, 1.0).astype(BF16)

    q2 = q_ref[0]
    lane_q = lax.broadcasted_iota(jnp.int32, (blk, LANE), 1)
    lane_k = lax.broadcasted_iota(jnp.int32, (nb, LANE), 1)
    n_iota = lax.broadcasted_iota(jnp.int32, (nb, blk), 0)
    row = lax.broadcasted_iota(jnp.int32, (blk, blk), 0)
    colk = lax.broadcasted_iota(jnp.int32, (blk, blk), 1)
    km = km_ref[...]
    start = pl.multiple_of(qi * blk, blk)
    qas, init = [], []
    heads = ((ka0_ref, va0_ref), (ka1_ref, va1_ref))
    for hd, (ka_ref, va_ref) in enumerate(heads):
        mine_q = (lane_q < HEAD_DIM) if hd == 0 else (lane_q >= HEAD_DIM)
        mine_k = (lane_k < HEAD_DIM) if hd == 0 else (lane_k >= HEAD_DIM)
        kmh = jnp.where(mine_k, km, 0.0)
        hi = kmh.astype(BF16)
        lo = (kmh - hi.astype(F32)).astype(BF16)
        st = (lax.dot_general(hi, q2, _NT, preferred_element_type=F32)
              + lax.dot_general(lo, q2, _NT, preferred_element_type=F32))
        s = jnp.where(n_iota < qi, st, -jnp.inf)
        pen = jnp.full((nb, blk), MASK_NEG, F32)
        for _ in range(MOBA_TOPK):
            mx = jnp.max(s, axis=0, keepdims=True)
            idx = jnp.min(jnp.where(s == mx, n_iota, nb), axis=0, keepdims=True)
            chosen = jnp.logical_and(n_iota == idx, mx > -jnp.inf)
            pen = jnp.where(chosen, 0.0, pen)
            s = jnp.where(chosen, -jnp.inf, s)
        n_lo = HEAD_DIM if hd == 0 else 0
        parts = [jnp.zeros((n_lo, blk), F32)] if n_lo else []
        parts += [pen, jnp.zeros((LANE - nb - n_lo, blk), F32)]
        pen_q = jnp.concatenate(parts, axis=0).T
        qas.append(jnp.where(mine_q, q2, pen_q.astype(BF16)))

        qd = jnp.where(mine_q, q2, jnp.zeros_like(q2))
        s0 = lax.dot_general(qd, ka_ref[pl.ds(start, blk), :], _NT, preferred_element_type=F32)
        s0 = jnp.where(colk <= row, s0, -jnp.inf)
        m0 = jnp.max(s0, axis=1, keepdims=True)
        p0 = jnp.exp(s0 - m0)
        init += [m0, _dot(p0.astype(BF16), va_ref[pl.ds(start, blk), :])]

    def past(j, carry):
        off = pl.multiple_of(j * span, span)
        new = []
        for hd, (ka_ref, va_ref) in enumerate(heads):
            m, acc = carry[2 * hd:2 * hd + 2]
            sn = lax.dot_general(qas[hd], ka_ref[pl.ds(off, span), :], _NT, preferred_element_type=F32)
            mn = jnp.maximum(m, jnp.max(sn, axis=1, keepdims=True))
            p = jnp.exp(sn - mn)
            new += [mn, jnp.exp(m - mn) * acc + _dot(p.astype(BF16), va_ref[pl.ds(off, span), :])]
        return tuple(new)

    fin = lax.fori_loop(0, (qi + 1) // 2, past, tuple(init))
    outs = [acc / pltpu.roll(acc, HEAD_DIM, 1) for acc in (fin[1], fin[3])]
    o_ref[0] = jnp.where(lane_q < HEAD_DIM, outs[0], outs[1]).astype(BF16)


def _moba_attn(q, k, v, batch, seq):
    nhp = D_MODEL // LANE
    nq = seq // MOBA_BLOCK
    kern = functools.partial(_moba_kernel, seq=seq)
    qspec = pl.BlockSpec((1, MOBA_BLOCK, LANE), lambda b, hp, qi: (b, qi, hp))
    kvspec = pl.BlockSpec((1, seq, LANE), lambda b, hp, qi: (b, 0, hp))
    return pl.pallas_call(
        kern,
        grid=(batch, nhp, nq),
        in_specs=[qspec, kvspec, kvspec],
        out_specs=qspec,
        out_shape=jax.ShapeDtypeStruct((batch, seq, D_MODEL), BF16),
        scratch_shapes=[pltpu.VMEM((seq, LANE), BF16)] * 4 + [pltpu.VMEM((seq // MOBA_BLOCK, LANE), F32)],
        compiler_params=_cparams("parallel", "parallel", "arbitrary"),
        name="moba_attn",
    )(q, k, v)


def _batch_column(ref, b, rows=None):
    tile = ref[...] if rows is None else ref[rows, :]
    lane = lax.broadcasted_iota(jnp.int32, tile.shape, 1)
    return jnp.sum(jnp.where(lane == b, tile, 0.0), axis=1, keepdims=True)


MOBA_DEC_PAGES = 4


def _moba_dec_kernel(pt_ref, qt_ref, knt_ref, vnt_ref, *refs, n_blk):
    k_refs, v_refs = refs[:MOBA_DEC_PAGES], refs[MOBA_DEC_PAGES:2 * MOBA_DEC_PAGES]
    o_ref, qb_ref, s_ref, vall_ref = refs[2 * MOBA_DEC_PAGES:]
    b, n = pl.program_id(0), pl.program_id(1)
    hshape = (N_HEADS, HEAD_DIM, PAGE_SIZE)

    @pl.when(n == 0)
    def _():
        qb_ref[...] = jnp.broadcast_to(_batch_column(qt_ref, b), (D_MODEL, PAGE_SIZE))

    qb = qb_ref[...].reshape(hshape)
    for j, (kr, vr) in enumerate(zip(k_refs, v_refs)):
        pg = MOBA_DEC_PAGES * n + j
        s_ref[pg] = jnp.sum(kr[0].reshape(hshape) * qb, axis=1, keepdims=True)
        vall_ref[pg] = vr[0]

    @pl.when(n == pl.num_programs(1) - 1)
    def _():
        cshape = (N_HEADS, HEAD_DIM, 1)
        qc = qb_ref[:, 0:1].reshape(cshape)
        knc = _batch_column(knt_ref, b).reshape(cshape)
        vnc = _batch_column(vnt_ref, b).reshape(cshape)
        s_own = jnp.sum(qc * knc, axis=1, keepdims=True)
        bsum, bmax = [], []
        for i in range(n_blk):
            a, c = s_ref[2 * i], s_ref[2 * i + 1]
            bsum.append(jnp.sum(a + c, axis=2, keepdims=True))
            bmax.append(jnp.max(jnp.maximum(a, c), axis=2, keepdims=True))
        sels = []
        for i in range(n_blk):
            rank = jnp.zeros((N_HEADS, 1, 1), F32)
            for m in range(n_blk):
                if m != i:
                    beats = (bsum[m] >= bsum[i]) if m < i else (bsum[m] > bsum[i])
                    rank = rank + jnp.where(beats, 1.0, 0.0)
            sels.append(rank < MOBA_TOPK)
        mx = s_own
        for i in range(n_blk):
            mx = jnp.maximum(mx, jnp.where(sels[i], bmax[i], -jnp.inf))
        w_own = jnp.exp(s_own - mx)
        lacc = jnp.zeros((N_HEADS, 1, PAGE_SIZE), F32)
        vacc = jnp.zeros(hshape, F32)
        for pg in range(2 * n_blk):
            p = jnp.where(sels[pg // 2], jnp.exp(s_ref[pg] - mx), 0.0)
            lacc = lacc + p
            vacc = vacc + p * vall_ref[pg].reshape(hshape)
        den = w_own + jnp.sum(lacc, axis=2, keepdims=True)
        o = (w_own * vnc + jnp.sum(vacc, axis=2, keepdims=True)) / den
        o_ref[0] = jnp.broadcast_to(o, hshape).reshape(D_MODEL, PAGE_SIZE)


def _moba_dec(qt, knt, vnt, cache_k, cache_v, page_table):
    b, n_pages = page_table.shape
    ppb = MOBA_BLOCK // PAGE_SIZE
    n_blk = n_pages // ppb
    pt = page_table.reshape(-1)
    vec = pl.BlockSpec((D_MODEL, b), lambda i, n, pt: (0, 0))

    def page(off):
        return pl.BlockSpec((1, D_MODEL, PAGE_SIZE),
                            lambda i, n, pt: (pt[i * n_pages + n * MOBA_DEC_PAGES + off], 0, 0))

    pages = [page(off) for off in range(MOBA_DEC_PAGES)]
    kern = functools.partial(_moba_dec_kernel, n_blk=n_blk)
    return pl.pallas_call(
        kern,
        grid_spec=pltpu.PrefetchScalarGridSpec(
            num_scalar_prefetch=1,
            grid=(b, n_pages // MOBA_DEC_PAGES),
            in_specs=[vec, vec, vec] + pages + pages,
            out_specs=pl.BlockSpec((1, D_MODEL, PAGE_SIZE), lambda i, n, pt: (i, 0, 0)),
            scratch_shapes=[pltpu.VMEM((D_MODEL, PAGE_SIZE), F32),
                            pltpu.VMEM((n_pages, N_HEADS, 1, PAGE_SIZE), F32),
                            pltpu.VMEM((n_pages, D_MODEL, PAGE_SIZE), F32)]),
        out_shape=jax.ShapeDtypeStruct((b, D_MODEL, PAGE_SIZE), F32),
        compiler_params=_cparams("parallel", "arbitrary"),
        name="moba_dec",
    )(pt, qt, knt, vnt, *([cache_k] * MOBA_DEC_PAGES), *([cache_v] * MOBA_DEC_PAGES))


S5_SLAB = 8
S5_SLAB_W = S5_DIM // S5_SLAB


def _s5_in_kernel(x_ref, g_ref, wre_ref, wim_ref, re_ref, im_ref):
    h = _rms(x_ref[...], g_ref[...]).astype(BF16)
    for j in range(S5_SLAB):
        hj = h[:, j * LANE:(j + 1) * LANE]
        re_ref[:, j, :] = _dot(hj, wre_ref[j])
        im_ref[:, j, :] = _dot(hj, wim_ref[j])


S5_TOKEN = (S5_SLAB, S5_SLAB_W)


def _s5_in(x, g, wre, wim):
    m = x.shape[0]
    tm = _row_tile(m)
    spec = pl.BlockSpec((tm,) + S5_TOKEN, lambda i: (i, 0, 0))
    return pl.pallas_call(
        _s5_in_kernel,
        grid=(m // tm,),
        in_specs=[pl.BlockSpec((tm, D_MODEL), lambda i: (i, 0)), _resident((1, D_MODEL)),
                  _resident((S5_SLAB, LANE, S5_SLAB_W)), _resident((S5_SLAB, LANE, S5_SLAB_W))],
        out_specs=[spec, spec],
        out_shape=[jax.ShapeDtypeStruct((m,) + S5_TOKEN, F32)] * 2,
        compiler_params=_cparams("parallel"),
        name="s5_in",
    )(x, g, wre, wim)


def _s5_scan_kernel(bre_ref, bim_ref, ar_ref, ai_ref, xre_ref, xim_ref, st_ref, *, batch, steps):
    @pl.when(pl.program_id(0) == 0)
    def _():
        st_ref[...] = jnp.zeros_like(st_ref)

    ar = ar_ref[...]
    ai = ai_ref[...]

    def step(t, carry):
        new = []
        for b in range(batch):
            xr, xi = carry[2 * b], carry[2 * b + 1]
            nr = ar * xr - ai * xi + bre_ref[b, t]
            ni = ar * xi + ai * xr + bim_ref[b, t]
            xre_ref[b, t] = nr
            xim_ref[b, t] = ni
            new += [nr, ni]
        return tuple(new)

    init = tuple(st_ref[i] for i in range(2 * batch))
    fin = lax.fori_loop(0, steps, step, init)
    for i in range(2 * batch):
        st_ref[i] = fin[i]


def _s5_scan(bu_re, bu_im, abar_re, abar_im, batch, seq):
    steps = 32
    shape = (batch, seq, S5_SLAB, S5_SLAB_W)
    spec = pl.BlockSpec((batch, steps, S5_SLAB, S5_SLAB_W), lambda t: (0, t, 0, 0))
    kern = functools.partial(_s5_scan_kernel, batch=batch, steps=steps)
    xre, xim = pl.pallas_call(
        kern,
        grid=(seq // steps,),
        in_specs=[spec, spec, _resident((S5_SLAB, S5_SLAB_W)), _resident((S5_SLAB, S5_SLAB_W))],
        out_specs=[spec, spec],
        out_shape=[jax.ShapeDtypeStruct(shape, F32)] * 2,
        scratch_shapes=[pltpu.VMEM((2 * batch, S5_SLAB, S5_SLAB_W), F32)],
        compiler_params=_cparams("arbitrary"),
        name="s5_scan",
    )(bu_re.reshape(shape), bu_im.reshape(shape), abar_re, abar_im)
    return xre.reshape((batch * seq,) + S5_TOKEN), xim.reshape((batch * seq,) + S5_TOKEN)


def _s5_step_kernel(bre_ref, bim_ref, hr_ref, hi_ref, ar_ref, ai_ref, xre_ref, xim_ref):
    ar, ai, hr, hi = ar_ref[...], ai_ref[...], hr_ref[...], hi_ref[...]
    xre_ref[...] = ar * hr - ai * hi + bre_ref[...]
    xim_ref[...] = ar * hi + ai * hr + bim_ref[...]


def _s5_step(bu_re, bu_im, h_re, h_im, abar_re, abar_im):
    m = bu_re.shape[0]
    full = _resident((m,) + S5_TOKEN)
    vec = _resident(S5_TOKEN)
    return pl.pallas_call(
        _s5_step_kernel,
        grid=(1,),
        in_specs=[full, full, full, full, vec, vec],
        out_specs=[pl.BlockSpec((m,) + S5_TOKEN, lambda i: (0, 0, 0))] * 2,
        out_shape=[jax.ShapeDtypeStruct((m,) + S5_TOKEN, F32)] * 2,
        compiler_params=_cparams("arbitrary"),
        name="s5_step",
    )(bu_re, bu_im, h_re, h_im, abar_re, abar_im)


def _s5_out_kernel(x_ref, g_ref, xre_ref, xim_ref, cre_ref, cim_ref, d_ref, wglu_ref, o_ref):
    x = x_ref[...]
    h = _rms(x, g_ref[...])
    ys = []
    for j in range(S5_SLAB):
        cols = slice(j * S5_SLAB_W, (j + 1) * S5_SLAB_W)
        ys.append(_dot(xre_ref[:, cols].astype(BF16), cre_ref[j]) - _dot(xim_ref[:, cols].astype(BF16), cim_ref[j]))
    y = jnp.concatenate(ys, axis=1) + d_ref[...] * h
    gl = _dot(jax.nn.gelu(y).astype(BF16), wglu_ref[...])
    o_ref[...] = x + gl[:, :D_MODEL] * jax.nn.sigmoid(gl[:, D_MODEL:])


def _s5_out(x, g, xre, xim, cre, cim, d, wglu):
    m = x.shape[0]
    tm = _row_tile(m)
    xre, xim = xre.reshape(m, S5_DIM), xim.reshape(m, S5_DIM)
    sspec = pl.BlockSpec((tm, S5_DIM), lambda i: (i, 0))
    return pl.pallas_call(
        _s5_out_kernel,
        grid=(m // tm,),
        in_specs=[pl.BlockSpec((tm, D_MODEL), lambda i: (i, 0)), _resident((1, D_MODEL)), sspec, sspec,
                  _resident((S5_SLAB, S5_SLAB_W, LANE)), _resident((S5_SLAB, S5_SLAB_W, LANE)),
                  _resident((1, D_MODEL)), _resident((D_MODEL, 2 * D_MODEL))],
        out_specs=pl.BlockSpec((tm, D_MODEL), lambda i: (i, 0)),
        out_shape=jax.ShapeDtypeStruct((m, D_MODEL), F32),
        compiler_params=_cparams("parallel"),
        name="s5_out",
    )(x, g, xre, xim, cre, cim, d, wglu)


def _s5_params(a_re, a_im, log_dt, b_re, b_im, c_re, c_im):
    dt = jnp.exp(log_dt)[:, None]
    mag = jnp.exp(a_re * dt)
    abar_re = mag * jnp.cos(a_im * dt)
    abar_im = mag * jnp.sin(a_im * dt)
    den = a_re * a_re + a_im * a_im
    f_re = ((abar_re - 1.0) * a_re + abar_im * a_im) / den
    f_im = (abar_im * a_re - (abar_re - 1.0) * a_im) / den
    bbar_re = f_re[..., None] * b_re - f_im[..., None] * b_im
    bbar_im = f_re[..., None] * b_im + f_im[..., None] * b_re
    eye = jnp.eye(S5_SLAB, dtype=F32)

    def in_w(bb):
        bb = bb.reshape(S5_SLAB, S5_SLAB, S5_STATE, S5_GROUP)
        w = jnp.einsum("jgpc,gh->jgchp", bb, eye)
        return w.reshape(S5_SLAB, LANE, S5_SLAB_W).astype(BF16)

    def out_w(cc):
        cc = cc.reshape(S5_SLAB, S5_SLAB, S5_GROUP, S5_STATE)
        w = jnp.einsum("jgcp,gh->jgphc", cc, eye)
        return w.reshape(S5_SLAB, S5_SLAB_W, LANE).astype(BF16)

    return dict(abar_re=abar_re.reshape(S5_SLAB, S5_SLAB_W), abar_im=abar_im.reshape(S5_SLAB, S5_SLAB_W),
                wre=in_w(bbar_re), wim=in_w(bbar_im), cre=out_w(c_re), cim=out_w(c_im))


DIL_TQ = 128


def _dil_kernel(q_ref, kp_ref, kc_ref, vp_ref, vc_ref, o_ref, lse_ref, *, tq):
    t = pl.program_id(2)
    lane = lax.broadcasted_iota(jnp.int32, (DIL_TQ, LANE), 1)
    row = lax.broadcasted_iota(jnp.int32, (DIL_TQ, 2 * DIL_TQ), 0)
    col = lax.broadcasted_iota(jnp.int32, (DIL_TQ, 2 * DIL_TQ), 1)
    band = jnp.logical_and(col >= row, col <= row + DIL_TAPS)
    first = jnp.logical_and(band, jnp.logical_or(t > 0, col >= DIL_TQ))
    for hp in range(D_MODEL // LANE):
        cols = slice(hp * LANE, (hp + 1) * LANE)
        q_all = q_ref[0, :, cols]
        kcat = jnp.concatenate([kp_ref[0, :, cols], kc_ref[0, :, cols]], axis=0)
        vcat = jnp.concatenate([vp_ref[0, :, cols], vc_ref[0, :, cols]], axis=0)
        for sub in range(tq // DIL_TQ):
            q = q_all[sub * DIL_TQ:(sub + 1) * DIL_TQ]
            keys = kcat[sub * DIL_TQ:(sub + 2) * DIL_TQ]
            vals = vcat[sub * DIL_TQ:(sub + 2) * DIL_TQ]
            ok = first if sub == 0 else band
            o_h, lse_h = [], []
            for hd in range(2):
                mine = (lane < HEAD_DIM) if hd == 0 else (lane >= HEAD_DIM)
                qh = jnp.where(mine, q, jnp.zeros_like(q))
                s = lax.dot_general(qh, keys, _NT, preferred_element_type=F32)
                s = jnp.where(ok, s, -jnp.inf)
                m = jnp.max(s, axis=1, keepdims=True)
                p = jnp.exp(s - m)
                l = jnp.sum(p, axis=1, keepdims=True)
                o_h.append(_dot(p.astype(BF16), vals) / l)
                lse_h.append(m + jnp.log(l))
            rows = slice(sub * DIL_TQ, (sub + 1) * DIL_TQ)
            o_ref[0, rows, cols] = jnp.where(lane < HEAD_DIM, o_h[0], o_h[1]).astype(BF16)
            lse_ref[0, rows, cols] = jnp.where(lane < HEAD_DIM, lse_h[0], lse_h[1])


def _dil_attn(q, k, v, g, dil, batch, seq):
    sd = seq // dil
    tq = min(2 * DIL_TQ, sd)
    qv = q.reshape(batch, sd, dil * D_MODEL)
    kv = k.reshape(batch, sd, dil * D_MODEL)
    vv = v.reshape(batch, sd, dil * D_MODEL)
    sub = tq // DIL_TQ
    cur = pl.BlockSpec((1, tq, D_MODEL), lambda b, r, t: (b, t, r))
    prev = pl.BlockSpec((1, DIL_TQ, D_MODEL), lambda b, r, t: (b, jnp.maximum(t * sub - 1, 0), r))
    ospec = cur
    kern = functools.partial(_dil_kernel, tq=tq)
    o, lse = pl.pallas_call(
        kern,
        grid=(batch, dil, sd // tq),
        in_specs=[cur, prev, cur, prev, cur],
        out_specs=[ospec, ospec],
        out_shape=[jax.ShapeDtypeStruct((batch, sd, dil * D_MODEL), BF16),
                   jax.ShapeDtypeStruct((batch, sd, dil * D_MODEL), F32)],
        compiler_params=_cparams("parallel", "parallel", "parallel"),
        name=f"dil_attn{g}",
    )(qv, kv, kv, vv, vv)
    return o, lse


def _dil_out_kernel(o0_ref, o1_ref, o2_ref, l0_ref, l1_ref, l2_ref, x_ref, w_ref, out_ref, stage_ref):
    reps = D_MODEL // LANE

    def natural(ref, dil):
        if dil == 1:
            return ref[...].astype(F32)
        n = ref.shape[0]
        for r in range(dil):
            for c in range(reps):
                lo = r * D_MODEL + c * LANE
                stage_ref[c, pl.ds(r, n, stride=dil), :] = ref[:, lo:lo + LANE].astype(F32)
        return jnp.concatenate([stage_ref[c] for c in range(reps)], axis=1)

    dils = [d for _, d in DIL_PAIRS]
    ls = [natural(r, d) for r, d in zip((l0_ref, l1_ref, l2_ref), dils)]
    mx = jnp.maximum(jnp.maximum(ls[0], ls[1]), ls[2])
    es = [jnp.exp(l - mx) for l in ls]
    den = es[0] + es[1] + es[2]
    mix = sum((e / den) * natural(r, d) for e, r, d in zip(es, (o0_ref, o1_ref, o2_ref), dils))
    out_ref[...] = x_ref[...] + _dot(mix.astype(BF16), w_ref[...])


def _dil_out(os_, ls, x, w, batch, seq):
    m = x.shape[0]
    tm = _row_tile(seq)
    npb = seq // tm
    spec = pl.BlockSpec((tm, D_MODEL), lambda i: (i, 0))
    gspecs = [pl.BlockSpec((None, tm // d, d * D_MODEL), lambda i: (i // npb, i % npb, 0)) for _, d in DIL_PAIRS]
    return pl.pallas_call(
        _dil_out_kernel,
        grid=(m // tm,),
        in_specs=gspecs + gspecs + [spec, _resident((D_MODEL, D_MODEL))],
        out_specs=spec,
        out_shape=jax.ShapeDtypeStruct((m, D_MODEL), F32),
        scratch_shapes=[pltpu.VMEM((D_MODEL // LANE, tm, LANE), F32)],
        compiler_params=_cparams("parallel"),
        name="dil_out",
    )(*os_, *ls, x, w)


WIN_HEADS = 4


def _win_dec_kernel(qt_ref, knt_ref, vnt_ref, k0_ref, v0_ref, k1_ref, v1_ref, k2_ref, v2_ref,
                    o_ref, nk0_ref, nv0_ref, nk1_ref, nv1_ref, nk2_ref, nv2_ref):
    b, hc = pl.program_id(0), pl.program_id(1)
    rows = WIN_HEADS * HEAD_DIM
    cshape = (WIN_HEADS, HEAD_DIM, 1)
    caches = ((k0_ref, v0_ref, nk0_ref, nv0_ref), (k1_ref, v1_ref, nk1_ref, nv1_ref),
              (k2_ref, v2_ref, nk2_ref, nv2_ref))
    cols, ss, sn = [], [], []
    for g, (kc, _, _, _) in enumerate(caches):
        win, dil = DIL_PAIRS[g]
        sl = pl.ds(pl.multiple_of(g * D_MODEL + hc * rows, rows), rows)
        qc = _batch_column(qt_ref, b, sl).reshape(cshape)
        knc = _batch_column(knt_ref, b, sl).reshape(cshape)
        vnc = _batch_column(vnt_ref, b, sl).reshape(cshape)
        cols.append((knc, vnc))
        s = jnp.sum(kc[0].reshape(WIN_HEADS, HEAD_DIM, win) * qc, axis=1, keepdims=True)
        r = lax.broadcasted_iota(jnp.int32, s.shape, 2)
        ss.append(jnp.where((r & (dil - 1)) == 0, s, -jnp.inf))
        sn.append(jnp.sum(qc * knc, axis=1, keepdims=True))
    mx = sn[0]
    for g in range(N_DIL):
        mx = jnp.maximum(mx, jnp.maximum(sn[g], jnp.max(ss[g], axis=2, keepdims=True)))
    den = jnp.zeros((WIN_HEADS, 1, 1), F32)
    num = jnp.zeros(cshape, F32)
    for g, (kc, vc, nk, nv) in enumerate(caches):
        win, _ = DIL_PAIRS[g]
        knc, vnc = cols[g]
        p = jnp.exp(ss[g] - mx)
        pn = jnp.exp(sn[g] - mx)
        den = den + jnp.sum(p, axis=2, keepdims=True) + pn
        num = num + jnp.sum(p * vc[0].reshape(WIN_HEADS, HEAD_DIM, win), axis=2, keepdims=True) + pn * vnc
        last = lax.broadcasted_iota(jnp.int32, (rows, win), 1) == win - 1
        nk[0] = jnp.where(last, knc.reshape(rows, 1), pltpu.roll(kc[0], win - 1, 1))
        nv[0] = jnp.where(last, vnc.reshape(rows, 1), pltpu.roll(vc[0], win - 1, 1))
    o_ref[0] = jnp.broadcast_to(num / den, (WIN_HEADS, HEAD_DIM, LANE)).reshape(rows, LANE)


def _win_dec(qt, knt, vnt, caches):
    b = qt.shape[1]
    rows = WIN_HEADS * HEAD_DIM
    vec = pl.BlockSpec((N_DIL * D_MODEL, b), lambda i, hc: (0, 0))
    cspecs, cshapes = [], []
    for win, _ in DIL_PAIRS:
        cspecs += [pl.BlockSpec((1, rows, win), lambda i, hc: (i, hc, 0))] * 2
        cshapes += [jax.ShapeDtypeStruct((b, D_MODEL, win), F32)] * 2
    outs = pl.pallas_call(
        _win_dec_kernel,
        grid=(b, D_MODEL // rows),
        in_specs=[vec, vec, vec] + cspecs,
        out_specs=[pl.BlockSpec((1, rows, LANE), lambda i, hc: (i, hc, 0))] + cspecs,
        out_shape=[jax.ShapeDtypeStruct((b, D_MODEL, LANE), F32)] + cshapes,
        compiler_params=_cparams("parallel", "parallel"),
        name="win_dec",
    )(qt, knt, vnt, *caches)
    return outs[0], outs[1:]


def _prep_weights(W):
    P = {}
    row = lambda a: a.reshape(1, -1)
    for i in range(4):
        wup = W["f_w_up"][i].astype(BF16)
        wcv = jnp.pad(W["f_w_conv"][i], ((0, SUBLANE - 3), (0, 0)))
        wdn = W["f_w_down"][i].astype(BF16)
        P[f"ffn{i}"] = (row(W["norm_ffn"][i]), wup, wcv, wdn, row(W["norm_ple"][i]),
                        W["ple_w_gate"][i].astype(BF16), W["ple_w_proj"][i].astype(BF16), row(W["norm_final"]))
    P["norm_mix"] = [row(W["norm_mix"][i]) for i in range(4)]
    P["a_w_in"] = W["a_w_in"][0].astype(BF16)
    P["a_b_in"] = row(W["a_b_in"][0])
    P["a_w_dw"] = jnp.pad(W["a_w_dw"][0], ((0, CONV_HALO - CONV_A_WIDTH), (0, 0)))
    P["a_b_dw"] = row(W["a_b_dw"][0])
    P["a_ln_g"] = row(W["a_ln_g"][0])
    P["a_ln_b"] = row(W["a_ln_b"][0])
    P["a_w_out"] = W["a_w_out"][0].astype(BF16)
    P["b_w_qkv"] = W["b_w_qkv"][0].astype(BF16)
    P["b_w_o"] = W["b_w_o"][0].astype(BF16)
    P["s5"] = _s5_params(W["c_a_re"][0], W["c_a_im"][0], W["c_log_dt"][0], W["c_b_re"][0], W["c_b_im"][0],
                         W["c_c_re"][0], W["c_c_im"][0])
    P["c_d"] = row(W["c_d"][0])
    P["c_w_glu"] = W["c_w_glu"][0].astype(BF16)
    P["d_w_qkv"] = W["d_w_qkv"][0].astype(BF16)
    P["d_w_o"] = W["d_w_o"][0].astype(BF16)
    return P


def _heads(a, batch, rows):
    return a.reshape(1, batch, rows, N_HEADS, HEAD_DIM)


def _heads_t(a, batch, rows):
    return a.reshape(batch, N_HEADS, HEAD_DIM, rows).transpose(0, 3, 1, 2)[None]


def _cache_t(c):
    n, rows = c.shape[0], c.shape[1]
    return c.transpose(0, 2, 3, 1).reshape(n, D_MODEL, rows)


def _prompt_trunk(x, p, P):
    batch, seq, _ = x.shape
    m = batch * seq
    x = x.reshape(m, D_MODEL)
    tabs = _rope_tables(jnp.arange(seq, dtype=jnp.int32))
    st = {}
    u = _conv_in(x, P["norm_mix"][0], P["a_w_in"], P["a_b_in"])
    st["conv_a"] = u.reshape(batch, seq, D_MODEL)[None, :, seq - (CONV_A_WIDTH - 1):]
    x = _conv_out(u, x, seq, P["a_w_dw"], P["a_b_dw"], P["a_ln_g"], P["a_ln_b"], P["a_w_out"])
    ffn_states = []
    x, fs = _ffn_prompt(x, p[0].reshape(m, PLE_DIM), P["ffn0"], batch, seq, False)
    ffn_states.append(fs)
    q, k, v, kt, vt = _qkv(x, P["norm_mix"][1], P["b_w_qkv"], tabs, 1, seq, batch)
    st["moba_k"] = _heads_t(kt, batch, seq)
    st["moba_v"] = _heads_t(vt, batch, seq)
    o = _moba_attn(q[0].reshape(batch, seq, D_MODEL), k[0].reshape(batch, seq, D_MODEL),
                   v[0].reshape(batch, seq, D_MODEL), batch, seq)
    x = _proj(o.reshape(m, D_MODEL), x, P["b_w_o"])
    x, fs = _ffn_prompt(x, p[1].reshape(m, PLE_DIM), P["ffn1"], batch, seq, False)
    ffn_states.append(fs)
    s5 = P["s5"]
    bu_re, bu_im = _s5_in(x, P["norm_mix"][2], s5["wre"], s5["wim"])
    xre, xim = _s5_scan(bu_re, bu_im, s5["abar_re"], s5["abar_im"], batch, seq)
    last = lambda a: a.reshape((batch, seq) + S5_TOKEN)[:, seq - 1].reshape(1, batch, S5_GROUPS, S5_STATE)
    st["s5_re"] = last(xre)
    st["s5_im"] = last(xim)
    x = _s5_out(x, P["norm_mix"][2], xre, xim, s5["cre"], s5["cim"], P["c_d"], P["c_w_glu"])
    x, fs = _ffn_prompt(x, p[2].reshape(m, PLE_DIM), P["ffn2"], batch, seq, False)
    ffn_states.append(fs)
    os_, ls = [], []
    for g, (win, dil) in enumerate(DIL_PAIRS):
        q, k, v, kt, vt = _qkv(x, P["norm_mix"][3], P["d_w_qkv"], tabs, 1, seq, batch, group0=g, dil=dil)
        keep = min(win, seq)
        st[f"win_k{g + 1}"] = _heads_t(kt[:, :, seq - keep:], batch, keep)
        st[f"win_v{g + 1}"] = _heads_t(vt[:, :, seq - keep:], batch, keep)
        o, lse = _dil_attn(q, k, v, g, dil, batch, seq)
        os_.append(o)
        ls.append(lse)
    x = _dil_out(os_, ls, x, P["d_w_o"], batch, seq)
    x, fs = _ffn_prompt(x, p[3].reshape(m, PLE_DIM), P["ffn3"], batch, seq, True)
    ffn_states.append(fs)
    st["ffn_conv"] = jnp.stack(ffn_states, 0)
    return x.reshape(batch, seq, D_MODEL), st


def _sample_trunk(x, p, P, S, past_len):
    b = x.shape[0]
    x = x.reshape(b, D_MODEL)
    tabs = _rope_tables(jnp.full((b,), past_len, dtype=jnp.int32))
    st = {}
    buf = S["conv_a"][0]
    x, u = _conv_dec(x, P["norm_mix"][0], P["a_w_in"], P["a_b_in"], buf.transpose(1, 0, 2), P["a_w_dw"],
                     P["a_b_dw"], P["a_ln_g"], P["a_ln_b"], P["a_w_out"])
    st["conv_a"] = jnp.concatenate([buf[:, 1:], u[:, None]], axis=1)[None]
    ffn_states = []
    x, fs = _ffn_dec(x, p[0].reshape(b, PLE_DIM), P["ffn0"], S["ffn_conv"][0], False)
    ffn_states.append(fs)
    q, k, v = _qkv(x, P["norm_mix"][1], P["b_w_qkv"], tabs, 1, b)
    st["moba_k"] = _heads(k, b, 1)
    st["moba_v"] = _heads(v, b, 1)
    o = _moba_dec(q.astype(F32).T, k.T, v.T, _cache_t(S["moba_k"][0]), _cache_t(S["moba_v"][0]),
                  S["page_table"])
    x = _proj(o[:, :, 0].astype(BF16), x, P["b_w_o"])
    x, fs = _ffn_dec(x, p[1].reshape(b, PLE_DIM), P["ffn1"], S["ffn_conv"][1], False)
    ffn_states.append(fs)
    s5 = P["s5"]
    bu_re, bu_im = _s5_in(x, P["norm_mix"][2], s5["wre"], s5["wim"])
    xre, xim = _s5_step(bu_re, bu_im, S["s5_re"][0].reshape((b,) + S5_TOKEN),
                        S["s5_im"][0].reshape((b,) + S5_TOKEN), s5["abar_re"], s5["abar_im"])
    st["s5_re"] = xre.reshape(1, b, S5_GROUPS, S5_STATE)
    st["s5_im"] = xim.reshape(1, b, S5_GROUPS, S5_STATE)
    x = _s5_out(x, P["norm_mix"][2], xre, xim, s5["cre"], s5["cim"], P["c_d"], P["c_w_glu"])
    x, fs = _ffn_dec(x, p[2].reshape(b, PLE_DIM), P["ffn2"], S["ffn_conv"][2], False)
    ffn_states.append(fs)
    q, k, v = _qkv(x, P["norm_mix"][3], P["d_w_qkv"], tabs, N_DIL, b)
    caches = []
    for g in range(N_DIL):
        caches += [_cache_t(S[f"win_k{g + 1}"][0]), _cache_t(S[f"win_v{g + 1}"][0])]
    o, new = _win_dec(q.astype(F32).T, k.T, v.T, caches)
    for g, (win, _) in enumerate(DIL_PAIRS):
        st[f"win_k{g + 1}"] = _heads_t(new[2 * g], b, win)
        st[f"win_v{g + 1}"] = _heads_t(new[2 * g + 1], b, win)
    x = _proj(o[:, :, 0].astype(BF16), x, P["d_w_o"])
    x, fs = _ffn_dec(x, p[3].reshape(b, PLE_DIM), P["ffn3"], S["ffn_conv"][3], True)
    ffn_states.append(fs)
    st["ffn_conv"] = jnp.stack(ffn_states, 0)
    return x.reshape(b, 1, D_MODEL), st


_STATE_KEYS = ("conv_a", "moba_k", "moba_v", "s5_re", "s5_im", "win_k1", "win_v1", "win_k2", "win_v2",
               "win_k3", "win_v3", "ffn_conv")


def kernel(x_prompt, x_sample, state_conv_a, cache_moba_k, cache_moba_v, state_s5_re, state_s5_im, cache_win_k1, cache_win_v1, cache_win_k2, cache_win_v2, cache_win_k3, cache_win_v3, state_ffn_conv, page_table, p_prompt, p_sample, norm_mix, norm_ffn, norm_ple, norm_final, a_w_in, a_b_in, a_w_dw, a_b_dw, a_ln_g, a_ln_b, a_w_out, b_w_qkv, b_w_o, c_a_re, c_a_im, c_log_dt, c_b_re, c_b_im, c_c_re, c_c_im, c_d, c_w_glu, d_w_qkv, d_w_o, f_w_up, f_w_conv, f_w_down, ple_w_proj, ple_w_gate):
    W = dict(norm_mix=norm_mix, norm_ffn=norm_ffn, norm_ple=norm_ple, norm_final=norm_final,
             a_w_in=a_w_in, a_b_in=a_b_in, a_w_dw=a_w_dw, a_b_dw=a_b_dw, a_ln_g=a_ln_g, a_ln_b=a_ln_b,
             a_w_out=a_w_out, b_w_qkv=b_w_qkv, b_w_o=b_w_o,
             c_a_re=c_a_re, c_a_im=c_a_im, c_log_dt=c_log_dt, c_b_re=c_b_re, c_b_im=c_b_im,
             c_c_re=c_c_re, c_c_im=c_c_im, c_d=c_d, c_w_glu=c_w_glu,
             d_w_qkv=d_w_qkv, d_w_o=d_w_o, f_w_up=f_w_up, f_w_conv=f_w_conv, f_w_down=f_w_down,
             ple_w_proj=ple_w_proj, ple_w_gate=ple_w_gate)
    S = dict(conv_a=state_conv_a, moba_k=cache_moba_k, moba_v=cache_moba_v, s5_re=state_s5_re, s5_im=state_s5_im,
             win_k1=cache_win_k1, win_v1=cache_win_v1, win_k2=cache_win_k2, win_v2=cache_win_v2,
             win_k3=cache_win_k3, win_v3=cache_win_v3, ffn_conv=state_ffn_conv, page_table=page_table)
    P = _prep_weights(W)
    past_len = page_table.shape[1] * PAGE_SIZE
    y_prompt, sp = _prompt_trunk(x_prompt, p_prompt, P)
    y_sample, ss = _sample_trunk(x_sample, p_sample, P, S, past_len)
    return (y_prompt, y_sample) + tuple(sp[n] for n in _STATE_KEYS) + tuple(ss[n] for n in _STATE_KEYS)
```
